```python
import math
import jax, jax.numpy as jnp
from jax import lax
import numpy as np


D_MODEL = 1024
BATCH = 16
SEQ = 4096
DEPTH = 2

N_MIXERS = 2
N_A = (DEPTH + 1) // 2
N_B = DEPTH // 2

SSM_EXPAND = 2
D_INNER = SSM_EXPAND * D_MODEL
SSM_HEAD_DIM = 64
SSM_HEADS = D_INNER // SSM_HEAD_DIM
SSM_GROUPS = 8
SSM_HPG = SSM_HEADS // SSM_GROUPS
SSM_STATE = 128
CONV_K = 4
CONV_DIM = D_INNER + 2 * SSM_GROUPS * SSM_STATE
SSM_IN_DIM = D_INNER + CONV_DIM + SSM_HEADS
NORM_GROUP = D_INNER // SSM_GROUPS
CHUNK = 128
DT_MIN = 0.001
DT_MAX = 0.1

MLA_HEADS = 16
Q_LORA = 384
KV_LORA = 256
QK_NOPE = 64
QK_ROPE = 32
V_DIM = 64
QK_DIM = QK_NOPE + QK_ROPE
MLA_IN_DIM = Q_LORA + KV_LORA + QK_ROPE
ROPE_THETA = 10000.0
Q_BLOCK = 128

D_FF = 2816
EPS = 1e-6

kernel_name = "hybrid_ssd_mla_macaron_trunk"


def rms_norm(x, w):
    xf = x.astype(jnp.float32)
    y = xf * lax.rsqrt(jnp.mean(xf * xf, axis=-1, keepdims=True) + EPS)
    return (y * w.astype(jnp.float32)).astype(x.dtype)


def swiglu(h, w_gate, w_up, w_down):
    return (jax.nn.silu(h @ w_gate) * (h @ w_up)) @ w_down


def causal_depthwise_conv(u, w, b):
    y = lax.conv_general_dilated(u, w[:, None, :].astype(u.dtype), window_strides=(1,),
                                 padding=[(CONV_K - 1, 0)],
                                 dimension_numbers=("NWC", "WIO", "NWC"),
                                 feature_group_count=u.shape[-1])
    return y + b


def ssd_chunked(xs, dt, a, bm, cm):
    b, s = xs.shape[:2]
    nc = s // CHUNK

    def to_chunks(t):
        return jnp.swapaxes(t.reshape((b, nc, CHUNK) + t.shape[2:]), 0, 1)

    mask = jnp.tril(jnp.ones((CHUNK, CHUNK), dtype=bool))[None, :, :, None, None]

    def step(state, inp):
        xc, dtc, bc, cc = inp
        acs = jnp.cumsum(dtc * a, axis=1)
        seg = acs[:, :, None] - acs[:, None, :]
        decay = jnp.exp(jnp.where(mask, seg, -jnp.inf))
        cb = jnp.einsum("blgn,bsgn->blsg", cc, bc)
        xdt = xc * dtc[..., None]
        y_diag = jnp.einsum("blsg,blsgr,bsgrp->blgrp", cb, decay, xdt)
        y_off = jnp.einsum("blgn,bgrpn->blgrp", cc, state) * jnp.exp(acs)[..., None]
        decay_to_end = jnp.exp(acs[:, -1:] - acs)
        new_state = (state * jnp.exp(acs[:, -1])[..., None, None]
                     + jnp.einsum("bsgn,bsgr,bsgrp->bgrpn", bc, decay_to_end, xdt))
        return new_state, y_diag + y_off

    state0 = jnp.zeros((b, SSM_GROUPS, SSM_HPG, SSM_HEAD_DIM, SSM_STATE), jnp.float32)
    _, ys = lax.scan(step, state0, (to_chunks(xs), to_chunks(dt), to_chunks(bm), to_chunks(cm)))
    return jnp.swapaxes(ys, 0, 1).reshape(xs.shape)


def mamba2_mixer(h, w_in, conv_w, conv_b, dt_bias, a_log, d_skip, norm_w, w_out):
    b, s, _ = h.shape
    proj = h @ w_in
    z = proj[..., :D_INNER]
    xbc = proj[..., D_INNER:D_INNER + CONV_DIM]
    dt = proj[..., D_INNER + CONV_DIM:]
    xbc = jax.nn.silu(causal_depthwise_conv(xbc, conv_w, conv_b))
    xs = xbc[..., :D_INNER].astype(jnp.float32).reshape(b, s, SSM_GROUPS, SSM_HPG, SSM_HEAD_DIM)
    bm = xbc[..., D_INNER:D_INNER + SSM_GROUPS * SSM_STATE].astype(jnp.float32).reshape(b, s, SSM_GROUPS, SSM_STATE)
    cm = xbc[..., D_INNER + SSM_GROUPS * SSM_STATE:].astype(jnp.float32).reshape(b, s, SSM_GROUPS, SSM_STATE)
    dt = jax.nn.softplus(dt.astype(jnp.float32) + dt_bias.astype(jnp.float32)).reshape(b, s, SSM_GROUPS, SSM_HPG)
    a = (-jnp.exp(a_log.astype(jnp.float32))).reshape(SSM_GROUPS, SSM_HPG)
    y = ssd_chunked(xs, dt, a, bm, cm)
    y = y + d_skip.astype(jnp.float32).reshape(SSM_GROUPS, SSM_HPG)[:, :, None] * xs
    g = (y.reshape(b, s, D_INNER) * jax.nn.silu(z.astype(jnp.float32))).reshape(b, s, SSM_GROUPS, NORM_GROUP)
    g = g * lax.rsqrt(jnp.mean(g * g, axis=-1, keepdims=True) + EPS)
    g = g.reshape(b, s, D_INNER) * norm_w.astype(jnp.float32)
    return g.astype(h.dtype) @ w_out


def rope_cos_sin(positions):
    inv_freq = 1.0 / (ROPE_THETA ** (jnp.arange(0, QK_ROPE, 2, dtype=jnp.float32) / QK_ROPE))
    ang = positions.astype(jnp.float32)[..., None] * inv_freq
    return jnp.cos(ang), jnp.sin(ang)


def apply_rope(x, cos, sin):
    xf = x.astype(jnp.float32)
    x1, x2 = xf[..., :QK_ROPE // 2], xf[..., QK_ROPE // 2:]
    return jnp.concatenate([x1 * cos - x2 * sin, x2 * cos + x1 * sin], axis=-1).astype(x.dtype)


def causal_block_attention(q_nope, q_rope, k_nope, k_rope, v):
    s = q_nope.shape[1]
    scale = QK_DIM ** -0.5
    outs = []
    for i in range(s // Q_BLOCK):
        q0 = i * Q_BLOCK
        kend = q0 + Q_BLOCK
        sc = (jnp.einsum("bqhd,bkhd->bhqk", q_nope[:, q0:kend], k_nope[:, :kend])
              + jnp.einsum("bqhr,bkr->bhqk", q_rope[:, q0:kend], k_rope[:, :kend]))
        sc = sc.astype(jnp.float32) * scale
        qi = q0 + jnp.arange(Q_BLOCK)
        ki = jnp.arange(kend)
        sc = jnp.where(ki[None, :] <= qi[:, None], sc, -jnp.inf)
        p = jax.nn.softmax(sc, axis=-1).astype(v.dtype)
        outs.append(jnp.einsum("bhqk,bkhd->bqhd", p, v[:, :kend]))
    return jnp.concatenate(outs, axis=1)


def mla_mixer(h, positions, w_in, q_a_norm, kv_a_norm, w_q_b, w_kv_b, q_norm, k_norm, w_out):
    b, s, _ = h.shape
    proj = h @ w_in
    cq = rms_norm(proj[..., :Q_LORA], q_a_norm)
    ckv = rms_norm(proj[..., Q_LORA:Q_LORA + KV_LORA], kv_a_norm)
    k_rope = proj[..., Q_LORA + KV_LORA:]
    q = (cq @ w_q_b).reshape(b, s, MLA_HEADS, QK_DIM)
    kv = (ckv @ w_kv_b).reshape(b, s, MLA_HEADS, QK_NOPE + V_DIM)
    q_nope, q_rope = q[..., :QK_NOPE], q[..., QK_NOPE:]
    k_nope, v = kv[..., :QK_NOPE], kv[..., QK_NOPE:]
    q_nope = rms_norm(q_nope, q_norm[:QK_NOPE])
    q_rope = rms_norm(q_rope, q_norm[QK_NOPE:])
    k_nope = rms_norm(k_nope, k_norm[:QK_NOPE])
    k_rope = rms_norm(k_rope, k_norm[QK_NOPE:])
    cos, sin = rope_cos_sin(positions)
    q_rope = apply_rope(q_rope, cos[:, :, None], sin[:, :, None])
    k_rope = apply_rope(k_rope, cos, sin)
    o = causal_block_attention(q_nope, q_rope, k_nope, k_rope, v)
    return o.reshape(b, s, MLA_HEADS * V_DIM) @ w_out


def setup_inputs(seed: int = 0) -> dict:
    key = jax.random.key(seed)
    ks = jax.random.split(key, 24)
    f32 = jnp.float32

    def nrm(k, shape, fan_in):
        return jax.random.normal(k, shape, f32) * fan_in ** -0.5

    def gain(k, shape):
        return 1.0 + 0.05 * jax.random.normal(k, shape, f32)

    x = jax.random.normal(ks[0], (BATCH, SEQ, D_MODEL), f32)
    offs = jax.random.randint(ks[1], (BATCH, 1), 0, 512, dtype=jnp.int32)
    positions = (jnp.arange(SEQ, dtype=jnp.int32)[None, :] + offs).astype(jnp.int32)

    norm_w = gain(ks[2], (DEPTH, 3, D_MODEL))
    ffn_w_gate = nrm(ks[3], (DEPTH, 2, D_MODEL, D_FF), D_MODEL)
    ffn_w_up = nrm(ks[4], (DEPTH, 2, D_MODEL, D_FF), D_MODEL)
    ffn_w_down = nrm(ks[5], (DEPTH, 2, D_FF, D_MODEL), D_FF)

    ssm_w_in = nrm(ks[6], (N_A, D_MODEL, SSM_IN_DIM), D_MODEL)
    ssm_conv_w = nrm(ks[7], (N_A, CONV_K, CONV_DIM), CONV_K)
    ssm_conv_b = 0.02 * jax.random.normal(ks[8], (N_A, CONV_DIM), f32)
    u = jax.random.uniform(ks[9], (N_A, SSM_HEADS), f32)
    dt0 = jnp.exp(u * (math.log(DT_MAX) - math.log(DT_MIN)) + math.log(DT_MIN))
    ssm_dt_bias = dt0 + jnp.log(-jnp.expm1(-dt0))
    ssm_a_log = jnp.log(jax.random.uniform(ks[10], (N_A, SSM_HEADS), f32, minval=1.0, maxval=16.0))
    ssm_d = 1.0 + 0.1 * jax.random.normal(ks[11], (N_A, SSM_HEADS), f32)
    ssm_norm_w = gain(ks[12], (N_A, D_INNER))
    ssm_w_out = nrm(ks[13], (N_A, D_INNER, D_MODEL), D_INNER)

    mla_w_in = nrm(ks[14], (N_B, D_MODEL, MLA_IN_DIM), D_MODEL)
    mla_q_a_norm = gain(ks[15], (N_B, Q_LORA))
    mla_kv_a_norm = gain(ks[16], (N_B, KV_LORA))
    mla_w_q_b = nrm(ks[17], (N_B, Q_LORA, MLA_HEADS * QK_DIM), Q_LORA)
    mla_w_kv_b = nrm(ks[18], (N_B, KV_LORA, MLA_HEADS * (QK_NOPE + V_DIM)), KV_LORA)
    mla_q_norm = gain(ks[19], (N_B, QK_DIM))
    mla_k_norm = gain(ks[20], (N_B, QK_DIM))
    mla_w_out = nrm(ks[21], (N_B, MLA_HEADS * V_DIM, D_MODEL), MLA_HEADS * V_DIM)

    return {"x": x, "positions": positions, "norm_w": norm_w,
            "ffn_w_gate": ffn_w_gate, "ffn_w_up": ffn_w_up, "ffn_w_down": ffn_w_down,
            "ssm_w_in": ssm_w_in, "ssm_conv_w": ssm_conv_w, "ssm_conv_b": ssm_conv_b,
            "ssm_dt_bias": ssm_dt_bias, "ssm_a_log": ssm_a_log, "ssm_d": ssm_d,
            "ssm_norm_w": ssm_norm_w, "ssm_w_out": ssm_w_out,
            "mla_w_in": mla_w_in, "mla_q_a_norm": mla_q_a_norm, "mla_kv_a_norm": mla_kv_a_norm,
            "mla_w_q_b": mla_w_q_b, "mla_w_kv_b": mla_w_kv_b, "mla_q_norm": mla_q_norm,
            "mla_k_norm": mla_k_norm, "mla_w_out": mla_w_out}


def reference(x, positions, norm_w, ffn_w_gate, ffn_w_up, ffn_w_down,
              ssm_w_in, ssm_conv_w, ssm_conv_b, ssm_dt_bias, ssm_a_log, ssm_d,
              ssm_norm_w, ssm_w_out,
              mla_w_in, mla_q_a_norm, mla_kv_a_norm, mla_w_q_b, mla_w_kv_b,
              mla_q_norm, mla_k_norm, mla_w_out):
    for i in range(DEPTH):
        x = x + 0.5 * swiglu(rms_norm(x, norm_w[i, 0]), ffn_w_gate[i, 0], ffn_w_up[i, 0], ffn_w_down[i, 0])
        h = rms_norm(x, norm_w[i, 1])
        j = i // N_MIXERS
        if i % N_MIXERS == 0:
            x = x + mamba2_mixer(h, ssm_w_in[j], ssm_conv_w[j], ssm_conv_b[j], ssm_dt_bias[j],
                                 ssm_a_log[j], ssm_d[j], ssm_norm_w[j], ssm_w_out[j])
        else:
            x = x + mla_mixer(h, positions, mla_w_in[j], mla_q_a_norm[j], mla_kv_a_norm[j],
                              mla_w_q_b[j], mla_w_kv_b[j], mla_q_norm[j], mla_k_norm[j], mla_w_out[j])
        x = x + 0.5 * swiglu(rms_norm(x, norm_w[i, 2]), ffn_w_gate[i, 1], ffn_w_up[i, 1], ffn_w_down[i, 1])
    return x
```

```python
import functools
import math

import jax
import jax.numpy as jnp
from jax import lax
from jax.experimental import pallas as pl
from jax.experimental.pallas import tpu as pltpu

F32 = jnp.float32
BF16 = jnp.bfloat16

D_MODEL = 1024
D_FF = 2816
EPS = 1e-6

SSM_HEADS = 32
SSM_HEAD_DIM = 64
SSM_GROUPS = 8
SSM_HPG = SSM_HEADS // SSM_GROUPS
SSM_STATE = 128
D_INNER = SSM_HEADS * SSM_HEAD_DIM
GROUP_COLS = SSM_HPG * SSM_HEAD_DIM
CONV_K = 4
CONV_DIM = D_INNER + 2 * SSM_GROUPS * SSM_STATE
CHUNK = 128

MLA_HEADS = 16
Q_LORA = 384
KV_LORA = 256
QK_NOPE = 64
QK_ROPE = 32
V_DIM = 64
QK_DIM = QK_NOPE + QK_ROPE
ROPE_THETA = 10000.0

LANES = 128
SUBLANES = 8
VMEM_LIMIT = 56 * 1024 * 1024

FFN_ROWS = 512
FFN_COLS = 256
SSM_IN_ROWS = 256
SSM_IN_COLS = 512
MLA_IN_ROWS = 256
ATTN_BLOCK = 256


def _dot(a, b):
    return jnp.dot(a, b, preferred_element_type=F32)


def _dot_nt(a, b):
    return lax.dot_general(a, b, (((1,), (1,)), ((), ())), preferred_element_type=F32)


def _dot_tn(a, b):
    return lax.dot_general(a, b, (((0,), (0,)), ((), ())), preferred_element_type=F32)


def _rms(x, w):
    return x * lax.rsqrt(jnp.mean(x * x, axis=-1, keepdims=True) + EPS) * w


def _silu(x):
    return x * jax.nn.sigmoid(x)


def _split2(x):
    hi = x.astype(BF16)
    lo = (x - hi.astype(F32)).astype(BF16)
    return hi, lo


def _split3(x):
    hi = x.astype(BF16)
    r = x - hi.astype(F32)
    mid = r.astype(BF16)
    lo = (r - mid.astype(F32)).astype(BF16)
    return hi, mid, lo


def _const_spec(shape):
    zeros = (0,) * len(shape)
    return pl.BlockSpec(shape, lambda *_: zeros, pipeline_mode=pl.Buffered(1))


def _params(sem):
    return pltpu.CompilerParams(dimension_semantics=sem, vmem_limit_bytes=VMEM_LIMIT)


def _ffn_body(x, nw_ref, wg_ref, wu_ref, wd_ref, o_ref, acc_ref):
    h = _rms(x, nw_ref[...]).astype(BF16)
    for c in range(D_FF // FFN_COLS):
        sl = slice(c * FFN_COLS, (c + 1) * FFN_COLS)
        g = _dot(h, wg_ref[:, sl])
        u = _dot(h, wu_ref[:, sl])
        a = (_silu(g) * u).astype(BF16)
        contrib = _dot(a, wd_ref[sl, :])
        if c == 0:
            acc_ref[...] = contrib
        else:
            acc_ref[...] += contrib
    o_ref[...] = x + 0.5 * acc_ref[...]


def _ffn_kernel(x_ref, nw_ref, wg_ref, wu_ref, wd_ref, o_ref, acc_ref):
    _ffn_body(x_ref[...], nw_ref, wg_ref, wu_ref, wd_ref, o_ref, acc_ref)


def _ffn_proj_kernel(x_ref, a_ref, wo_ref, nw_ref, wg_ref, wu_ref, wd_ref, o_ref, acc_ref):
    x = x_ref[...] + _dot(a_ref[...], wo_ref[...])
    _ffn_body(x, nw_ref, wg_ref, wu_ref, wd_ref, o_ref, acc_ref)


def _ffn(x, nw, wg, wu, wd, attn=None, wo=None):
    t = x.shape[0]
    rows = min(FFN_ROWS, t)
    row_spec = pl.BlockSpec((rows, D_MODEL), lambda i: (i, 0))
    w_specs = [_const_spec((1, D_MODEL)), _const_spec((D_MODEL, D_FF)),
               _const_spec((D_MODEL, D_FF)), _const_spec((D_FF, D_MODEL))]
    if attn is None:
        kern, ins, specs = _ffn_kernel, (x,), [row_spec]
    else:
        kern, ins = _ffn_proj_kernel, (x, attn, wo)
        specs = [row_spec, row_spec, _const_spec((D_MODEL, D_MODEL))]
    return pl.pallas_call(
        kern,
        grid=(t // rows,),
        in_specs=specs + w_specs,
        out_specs=row_spec,
        out_shape=jax.ShapeDtypeStruct((t, D_MODEL), F32),
        scratch_shapes=[pltpu.VMEM((rows, D_MODEL), F32)],
        compiler_params=_params(("parallel",)),
        name="ffn" if attn is None else "ffn_proj",
    )(*ins, nw, wg, wu, wd)


def _ssm_in_kernel(x_ref, nw_ref, wz_ref, wxbc_ref, wdt_ref, cw_ref, cb_ref, dtb_ref,
                   z_ref, xbc_ref, dt_ref, ubuf):
    rows = x_ref.shape[0]
    h = _rms(x_ref[...], nw_ref[...]).astype(BF16)
    z_ref[...] = _dot(h, wz_ref[...]).astype(BF16)
    dtv = _dot(h, wdt_ref[...]) + dtb_ref[...]
    dt_ref[...] = jnp.maximum(dtv, 0.0) + jnp.log1p(jnp.exp(-jnp.abs(dtv)))

    @pl.when(pl.program_id(1) == 0)
    def _():
        ubuf[0:SUBLANES, :] = jnp.zeros((SUBLANES, CONV_DIM), F32)

    for c in range(CONV_DIM // SSM_IN_COLS):
        sl = slice(c * SSM_IN_COLS, (c + 1) * SSM_IN_COLS)
        ubuf[SUBLANES:SUBLANES + rows, sl] = _dot(h, wxbc_ref[:, sl])
        acc = jnp.broadcast_to(cb_ref[:, sl], (rows, SSM_IN_COLS))
        for k in range(CONV_K):
            off = SUBLANES - (CONV_K - 1) + k
            acc = acc + cw_ref[k:k + 1, sl] * ubuf[off:off + rows, sl]
        xbc_ref[:, sl] = _silu(acc).astype(BF16)
    ubuf[0:SUBLANES, :] = ubuf[rows:rows + SUBLANES, :]


def _ssm_in(x, bsz, seq, nw, wz, wxbc, wdt, cw, cb, dtb):
    t = x.shape[0]
    rows = min(SSM_IN_ROWS, seq)
    ns = seq // rows
    row = lambda w: pl.BlockSpec((rows, w), lambda b, s: (b * ns + s, 0))
    return pl.pallas_call(
        _ssm_in_kernel,
        grid=(bsz, ns),
        in_specs=[row(D_MODEL), _const_spec((1, D_MODEL)), _const_spec((D_MODEL, D_INNER)),
                  _const_spec((D_MODEL, CONV_DIM)), _const_spec((D_MODEL, LANES)),
                  _const_spec((CONV_K, CONV_DIM)), _const_spec((1, CONV_DIM)),
                  _const_spec((1, LANES))],
        out_specs=[row(D_INNER), row(CONV_DIM), row(LANES)],
        out_shape=[jax.ShapeDtypeStruct((t, D_INNER), BF16),
                   jax.ShapeDtypeStruct((t, CONV_DIM), BF16),
                   jax.ShapeDtypeStruct((t, LANES), F32)],
        scratch_shapes=[pltpu.VMEM((rows + SUBLANES, CONV_DIM), F32)],
        compiler_params=_params(("parallel", "arbitrary")),
        name="ssm_in",
    )(x, nw, wz, wxbc, wdt, cw, cb, dtb)


def _ssd_kernel(x_ref, z_ref, xbc_ref, dt_ref, a_ref, d_ref, nw_ref, tril_ref, exp_ref,
                wo_ref, o_ref, state, ybuf):
    L = CHUNK

    @pl.when(pl.program_id(1) == 0)
    def _():
        state[...] = jnp.zeros(state.shape, F32)

    dt = dt_ref[...]
    dta = dt * a_ref[...]
    hi, mid, lo = _split3(dta)
    cs = _dot(tril_ref[...], jnp.concatenate([hi, mid, lo], axis=1))
    acs = cs[:, :LANES] + cs[:, LANES:2 * LANES] + cs[:, 2 * LANES:]
    acs_last = acs[L - 1:L, :]
    eacs = jnp.exp(acs)
    wgt = dt * jnp.exp(acs_last - acs)
    acs_t = acs.T
    dt_t = dt.T

    w_hi, w_lo = _split2(wgt)
    e_hi, e_lo = _split2(eacs)
    stacked = jnp.concatenate([w_hi, w_lo, e_hi, e_lo], axis=0)

    row = lax.broadcasted_iota(jnp.int32, (L, L), 0)
    col = lax.broadcasted_iota(jnp.int32, (L, L), 1)
    causal = col <= row

    for g in range(SSM_GROUPS):
        gsl = slice(g * GROUP_COLS, (g + 1) * GROUP_COLS)
        ex = _dot(stacked, exp_ref[:, gsl])
        wgt_x = ex[0:L] + ex[L:2 * L]
        eacs_x = ex[2 * L:3 * L] + ex[3 * L:4 * L]
        xs_b = xbc_ref[:, gsl]
        xs = xs_b.astype(F32)
        bm = xbc_ref[:, D_INNER + g * SSM_STATE:D_INNER + (g + 1) * SSM_STATE]
        cm = xbc_ref[:, D_INNER + SSM_GROUPS * SSM_STATE + g * SSM_STATE:
                     D_INNER + SSM_GROUPS * SSM_STATE + (g + 1) * SSM_STATE]
        cb = _dot_nt(cm, bm)
        st = state[g]
        y_off = _dot(cm, st.astype(BF16)) * eacs_x
        state[g] = st * eacs_x[L - 1:L, :] + _dot_tn(bm, (xs * wgt_x).astype(BF16))
        y_heads = []
        for r in range(SSM_HPG):
            hd = g * SSM_HPG + r
            seg = acs[:, hd:hd + 1] - acs_t[hd:hd + 1, :]
            decay = jnp.exp(jnp.where(causal, seg, -jnp.inf))
            m = (cb * decay * dt_t[hd:hd + 1, :]).astype(BF16)
            y_heads.append(_dot(m, xs_b[:, r * SSM_HEAD_DIM:(r + 1) * SSM_HEAD_DIM]))
        y = jnp.concatenate(y_heads, axis=1) + y_off + d_ref[:, gsl] * xs
        gt = y * _silu(z_ref[:, gsl].astype(F32))
        gt = gt * lax.rsqrt(jnp.mean(gt * gt, axis=-1, keepdims=True) + EPS)
        ybuf[:, gsl] = (gt * nw_ref[:, gsl]).astype(BF16)
    o_ref[...] = x_ref[...] + _dot(ybuf[...], wo_ref[...])


def _ssd(x, bsz, seq, z, xbc, dt, a_row, d_row, nw, tril, expand, wo):
    t = x.shape[0]
    nc = seq // CHUNK
    row = lambda w: pl.BlockSpec((CHUNK, w), lambda b, c: (b * nc + c, 0))
    return pl.pallas_call(
        _ssd_kernel,
        grid=(bsz, nc),
        in_specs=[row(D_MODEL), row(D_INNER), row(CONV_DIM), row(LANES),
                  _const_spec((1, LANES)), _const_spec((1, D_INNER)), _const_spec((1, D_INNER)),
                  _const_spec((CHUNK, CHUNK)), _const_spec((LANES, D_INNER)),
                  _const_spec((D_INNER, D_MODEL))],
        out_specs=row(D_MODEL),
        out_shape=jax.ShapeDtypeStruct((t, D_MODEL), F32),
        scratch_shapes=[pltpu.VMEM((SSM_GROUPS, SSM_STATE, GROUP_COLS), F32),
                        pltpu.VMEM((CHUNK, D_INNER), BF16)],
        compiler_params=_params(("parallel", "arbitrary")),
        name="ssd",
    )(x, z, xbc, dt, a_row, d_row, nw, tril, expand, wo)


def _group_rstd(sq, group):
    lane = lax.broadcasted_iota(jnp.int32, sq.shape, 1)
    res = jnp.zeros_like(sq)
    for gi in range(LANES // group):
        msk = (lane >= gi * group) & (lane < (gi + 1) * group)
        s = jnp.sum(jnp.where(msk, sq, 0.0), axis=-1, keepdims=True)
        res = jnp.where(msk, lax.rsqrt(s * (1.0 / group) + EPS), res)
    return res


def _mla_in_kernel(x_ref, pos_ref, nw_ref, wcq_ref, wckv_ref, wkr_ref, wkrs_ref,
                   qan_ref, kvan_ref, wqn_ref, wqr_ref, wqrs_ref, wkn_ref, wv_ref,
                   gqn_ref, gkn_ref, gqr_ref, gqrs_ref, gkr_ref, gkrs_ref, freq_ref, sgn_ref,
                   qn_ref, qr_ref, kn_ref, kr_ref, v_ref):
    h = _rms(x_ref[...], nw_ref[...]).astype(BF16)
    cq = _rms(_dot(h, wcq_ref[...]), qan_ref[...]).astype(BF16)
    ckv = _rms(_dot(h, wckv_ref[...]), kvan_ref[...]).astype(BF16)

    ang = pos_ref[...].astype(F32) * freq_ref[...]
    cos = jnp.cos(ang)
    sin = jnp.sin(ang) * sgn_ref[...]

    qscale = QK_DIM ** -0.5 * math.log2(math.e)

    v_ref[...] = _dot(ckv, wv_ref[...]).astype(BF16)
    for c in range(MLA_HEADS * QK_NOPE // LANES):
        sl = slice(c * LANES, (c + 1) * LANES)
        qn = _dot(cq, wqn_ref[:, sl])
        qn_ref[:, sl] = (qn * _group_rstd(qn * qn, QK_NOPE) * (gqn_ref[...] * qscale)).astype(BF16)
        kn = _dot(ckv, wkn_ref[:, sl])
        kn_ref[:, sl] = (kn * _group_rstd(kn * kn, QK_NOPE) * gkn_ref[...]).astype(BF16)
    for c in range(MLA_HEADS * QK_ROPE // LANES):
        sl = slice(c * LANES, (c + 1) * LANES)
        qr = _dot(cq, wqr_ref[:, sl])
        qrs = _dot(cq, wqrs_ref[:, sl])
        rot = qr * gqr_ref[...] * cos + qrs * gqrs_ref[...] * sin
        qr_ref[:, sl] = (rot * _group_rstd(qr * qr, QK_ROPE) * qscale).astype(BF16)
    kr = _dot(h, wkr_ref[...])
    krs = _dot(h, wkrs_ref[...])
    rot = kr * gkr_ref[...] * cos + krs * gkrs_ref[...] * sin
    kr_ref[...] = (rot * _group_rstd(kr * kr, QK_ROPE)).astype(BF16)


def _mla_in(x, pos, ws):
    t = x.shape[0]
    rows = min(MLA_IN_ROWS, t)
    row = lambda w: pl.BlockSpec((rows, w), lambda i: (i, 0))
    nope = MLA_HEADS * QK_NOPE
    rope = MLA_HEADS * QK_ROPE
    shapes = [(1, D_MODEL), (D_MODEL, Q_LORA), (D_MODEL, KV_LORA), (D_MODEL, LANES), (D_MODEL, LANES),
              (1, Q_LORA), (1, KV_LORA), (Q_LORA, nope), (Q_LORA, rope), (Q_LORA, rope),
              (KV_LORA, nope), (KV_LORA, MLA_HEADS * V_DIM)] + [(1, LANES)] * 8
    return pl.pallas_call(
        _mla_in_kernel,
        grid=(t // rows,),
        in_specs=[row(D_MODEL), row(1)] + [_const_spec(s) for s in shapes],
        out_specs=[row(nope), row(rope), row(nope), row(LANES), row(MLA_HEADS * V_DIM)],
        out_shape=[jax.ShapeDtypeStruct((t, nope), BF16), jax.ShapeDtypeStruct((t, rope), BF16),
                   jax.ShapeDtypeStruct((t, nope), BF16), jax.ShapeDtypeStruct((t, LANES), BF16),
                   jax.ShapeDtypeStruct((t, MLA_HEADS * V_DIM), BF16)],
        compiler_params=_params(("parallel",)),
        name="mla_in",
    )(x, pos, *ws)


def _attn_kernel(qn_ref, qr_ref, kn_ref, kr_ref, v_ref, o_ref, m_ref, l_ref, acc_ref):
    blk = ATTN_BLOCK
    p = pl.program_id(1)
    i = pl.program_id(2)
    lane = lax.broadcasted_iota(jnp.int32, (blk, LANES), 1)
    qn = qn_ref[...]
    qr = qr_ref[...]
    zero = jnp.zeros_like(qn)
    rope_lo = (p % 2) * (2 * QK_ROPE)
    qs = []
    for hh in range(2):
        nope_part = jnp.where((lane >= hh * QK_NOPE) & (lane < (hh + 1) * QK_NOPE), qn, zero)
        lo = rope_lo + hh * QK_ROPE
        rope_part = jnp.where((lane >= lo) & (lane < lo + QK_ROPE), qr, zero)
        qs.append(jnp.concatenate([nope_part, rope_part], axis=1))

    m_ref[...] = jnp.full(m_ref.shape, -jnp.inf, F32)
    l_ref[...] = jnp.zeros(l_ref.shape, F32)
    acc_ref[...] = jnp.zeros(acc_ref.shape, F32)

    def step(j, masked):
        ks = pl.ds(pl.multiple_of(j * blk, blk), blk)
        kfull = jnp.concatenate([kn_ref[ks, :], kr_ref[ks, :]], axis=1)
        vv = v_ref[ks, :]
        for hh in range(2):
            s = _dot_nt(qs[hh], kfull)
            if masked:
                r = lax.broadcasted_iota(jnp.int32, (blk, blk), 0)
                c = lax.broadcasted_iota(jnp.int32, (blk, blk), 1)
                s = jnp.where(c <= r, s, -jnp.inf)
            m_old = m_ref[hh]
            m_new = jnp.maximum(m_old, jnp.max(s, axis=-1, keepdims=True))
            alpha = jnp.exp2(m_old - m_new)
            pr = jnp.exp2(s - m_new)
            l_ref[hh] = alpha * l_ref[hh] + jnp.sum(pr, axis=-1, keepdims=True)
            acc_ref[hh] = alpha * acc_ref[hh] + _dot(pr.astype(BF16), vv)
            m_ref[hh] = m_new

    def body(j, carry):
        step(j, False)
        return carry

    lax.fori_loop(0, i, body, 0)
    step(i, True)

    o0 = acc_ref[0] / l_ref[0]
    o1 = acc_ref[1] / l_ref[1]
    o_ref[...] = jnp.where(lane < V_DIM, o0, o1).astype(BF16)


def _attn(qn, qr, kn, kr, v, bsz, seq):
    t = qn.shape[0]
    blk = min(ATTN_BLOCK, seq)
    assert blk == ATTN_BLOCK or seq < ATTN_BLOCK
    nq = seq // blk
    npair = MLA_HEADS // 2
    qspec = pl.BlockSpec((blk, LANES), lambda b, p, i: (b * nq + i, p))
    qrspec = pl.BlockSpec((blk, LANES), lambda b, p, i: (b * nq + i, p // 2))
    kspec = pl.BlockSpec((seq, LANES), lambda b, p, i: (b, p))
    krspec = pl.BlockSpec((seq, LANES), lambda b, p, i: (b, 0))
    return pl.pallas_call(
        _attn_kernel,
        grid=(bsz, npair, nq),
        in_specs=[qspec, qrspec, kspec, krspec, kspec],
        out_specs=qspec,
        out_shape=jax.ShapeDtypeStruct((t, MLA_HEADS * V_DIM), BF16),
        scratch_shapes=[pltpu.VMEM((2, blk, 1), F32), pltpu.VMEM((2, blk, 1), F32),
                        pltpu.VMEM((2, blk, LANES), F32)],
        compiler_params=_params(("parallel", "parallel", "arbitrary")),
        name="attn",
    )(qn, qr, kn, kr, v)


def _row(v, width=None):
    v = v.astype(F32).reshape(1, -1)
    if width is not None and v.shape[1] < width:
        v = jnp.pad(v, ((0, 0), (0, width - v.shape[1])))
    return v


def _swap_halves(w):
    k, n = w.shape
    w = w.reshape(k, n // QK_ROPE, 2, QK_ROPE // 2)
    return w[:, :, ::-1, :].reshape(k, n)


def kernel(x, positions, norm_w, ffn_w_gate, ffn_w_up, ffn_w_down, ssm_w_in, ssm_conv_w, ssm_conv_b, ssm_dt_bias, ssm_a_log, ssm_d, ssm_norm_w, ssm_w_out, mla_w_in, mla_q_a_norm, mla_kv_a_norm, mla_w_q_b, mla_w_kv_b, mla_q_norm, mla_k_norm, mla_w_out):
    bsz, seq, _ = x.shape
    t = bsz * seq
    depth = norm_w.shape[0]
    xf = x.reshape(t, D_MODEL)

    def ffn(xin, i, k, attn=None, wo=None):
        return _ffn(xin, _row(norm_w[i, 0 if k == 0 else 2]),
                    ffn_w_gate[i, k].astype(BF16), ffn_w_up[i, k].astype(BF16),
                    ffn_w_down[i, k].astype(BF16), attn, wo)

    for i in range(depth):
        xf = ffn(xf, i, 0)
        j = i // 2
        if i % 2 == 0:
            w_in = ssm_w_in[j]
            wz = w_in[:, :D_INNER].astype(BF16)
            wxbc = w_in[:, D_INNER:D_INNER + CONV_DIM].astype(BF16)
            wdt = jnp.pad(w_in[:, D_INNER + CONV_DIM:], ((0, 0), (0, LANES - SSM_HEADS))).astype(BF16)
            z, xbc, dt = _ssm_in(xf, bsz, seq, _row(norm_w[i, 1]), wz, wxbc, wdt,
                                 ssm_conv_w[j].astype(F32), _row(ssm_conv_b[j]),
                                 _row(ssm_dt_bias[j], LANES))
            a_row = _row(-jnp.exp(ssm_a_log[j].astype(F32)), LANES)
            d_row = _row(jnp.repeat(ssm_d[j].astype(F32), SSM_HEAD_DIM))
            tril = jnp.tril(jnp.ones((CHUNK, CHUNK), F32)).astype(BF16)
            head_of_col = jnp.arange(D_INNER, dtype=jnp.int32) // SSM_HEAD_DIM
            expand = (jnp.arange(LANES, dtype=jnp.int32)[:, None] == head_of_col[None, :]).astype(BF16)
            xf = _ssd(xf, bsz, seq, z, xbc, dt, a_row, d_row, _row(ssm_norm_w[j]), tril, expand,
                      ssm_w_out[j].astype(BF16))
            xf = ffn(xf, i, 1)
        else:
            w_in = mla_w_in[j]
            wcq = w_in[:, :Q_LORA].astype(BF16)
            wckv = w_in[:, Q_LORA:Q_LORA + KV_LORA].astype(BF16)
            wkr = jnp.tile(w_in[:, Q_LORA + KV_LORA:], (1, LANES // QK_ROPE))
            wq = mla_w_q_b[j].reshape(Q_LORA, MLA_HEADS, QK_DIM)
            wqn = wq[:, :, :QK_NOPE].reshape(Q_LORA, MLA_HEADS * QK_NOPE).astype(BF16)
            wqr = wq[:, :, QK_NOPE:].reshape(Q_LORA, MLA_HEADS * QK_ROPE)
            wkv = mla_w_kv_b[j].reshape(KV_LORA, MLA_HEADS, QK_NOPE + V_DIM)
            wkn = wkv[:, :, :QK_NOPE].reshape(KV_LORA, MLA_HEADS * QK_NOPE).astype(BF16)
            wv = wkv[:, :, QK_NOPE:].reshape(KV_LORA, MLA_HEADS * V_DIM).astype(BF16)
            qg, kg = mla_q_norm[j].astype(F32), mla_k_norm[j].astype(F32)
            tile_n = lambda g: jnp.tile(g, LANES // g.shape[0]).reshape(1, LANES)
            swap = lambda g: jnp.concatenate([g[QK_ROPE // 2:], g[:QK_ROPE // 2]])
            inv_freq = 1.0 / (ROPE_THETA ** (jnp.arange(0, QK_ROPE, 2, dtype=F32) / QK_ROPE))
            freq = jnp.tile(inv_freq, LANES // (QK_ROPE // 2)).reshape(1, LANES)
            sgn = jnp.tile(jnp.concatenate([-jnp.ones(QK_ROPE // 2, F32), jnp.ones(QK_ROPE // 2, F32)]),
                           LANES // QK_ROPE).reshape(1, LANES)
            ws = [_row(norm_w[i, 1]), wcq, wckv, wkr.astype(BF16), _swap_halves(wkr).astype(BF16),
                  _row(mla_q_a_norm[j]), _row(mla_kv_a_norm[j]), wqn, wqr.astype(BF16),
                  _swap_halves(wqr).astype(BF16), wkn, wv,
                  tile_n(qg[:QK_NOPE]), tile_n(kg[:QK_NOPE]), tile_n(qg[QK_NOPE:]),
                  tile_n(swap(qg[QK_NOPE:])), tile_n(kg[QK_NOPE:]), tile_n(swap(kg[QK_NOPE:])),
                  freq, sgn]
            qn, qr, kn, kr, v = _mla_in(xf, positions.reshape(t, 1).astype(jnp.int32), ws)
            o = _attn(qn, qr, kn, kr, v, bsz, seq)
            xf = ffn(xf, i, 1, attn=o, wo=mla_w_out[j].astype(BF16))
    return xf.reshape(bsz, seq, D_MODEL)
```

```python
import functools
import math

import jax
import jax.numpy as jnp
from jax import lax
from jax.experimental import pallas as pl
from jax.experimental.pallas import tpu as pltpu

F32 = jnp.float32
BF16 = jnp.bfloat16

D_MODEL = 1024
D_FF = 2816
EPS = 1e-6

SSM_HEADS = 32
SSM_HEAD_DIM = 64
SSM_GROUPS = 8
SSM_HPG = SSM_HEADS // SSM_GROUPS
SSM_STATE = 128
D_INNER = SSM_HEADS * SSM_HEAD_DIM
GROUP_COLS = SSM_HPG * SSM_HEAD_DIM
CONV_K = 4
CONV_DIM = D_INNER + 2 * SSM_GROUPS * SSM_STATE
CHUNK = 128

MLA_HEADS = 16
Q_LORA = 384
KV_LORA = 256
QK_NOPE = 64
QK_ROPE = 32
V_DIM = 64
QK_DIM = QK_NOPE + QK_ROPE
ROPE_THETA = 10000.0

LANES = 128
SUBLANES = 8
VMEM_LIMIT = 56 * 1024 * 1024

FFN_ROWS = 512
FFN_COLS = 256
SSM_IN_ROWS = 256
SSM_IN_COLS = 512
MLA_IN_ROWS = 256
ATTN_BLOCK = 512
ATTN_QSUB = 256


def _dot(a, b):
    return jnp.dot(a, b, preferred_element_type=F32)


def _dot_nt(a, b):
    return lax.dot_general(a, b, (((1,), (1,)), ((), ())), preferred_element_type=F32)


def _dot_tn(a, b):
    return lax.dot_general(a, b, (((0,), (0,)), ((), ())), preferred_element_type=F32)


def _rms(x, w):
    return x * lax.rsqrt(jnp.mean(x * x, axis=-1, keepdims=True) + EPS) * w


def _silu(x):
    return x * jax.nn.sigmoid(x)


def _split2(x):
    hi = x.astype(BF16)
    lo = (x - hi.astype(F32)).astype(BF16)
    return hi, lo


def _split3(x):
    hi = x.astype(BF16)
    r = x - hi.astype(F32)
    mid = r.astype(BF16)
    lo = (r - mid.astype(F32)).astype(BF16)
    return hi, mid, lo


def _const_spec(shape):
    zeros = (0,) * len(shape)
    return pl.BlockSpec(shape, lambda *_: zeros, pipeline_mode=pl.Buffered(1))


def _params(sem):
    return pltpu.CompilerParams(dimension_semantics=sem, vmem_limit_bytes=VMEM_LIMIT)


def _ffn_body(x, nw_ref, wg_ref, wu_ref, wd_ref, o_ref, acc_ref):
    h = _rms(x, nw_ref[...]).astype(BF16)
    for c in range(D_FF // FFN_COLS):
        sl = slice(c * FFN_COLS, (c + 1) * FFN_COLS)
        g = _dot(h, wg_ref[:, sl])
        u = _dot(h, wu_ref[:, sl])
        a = (_silu(g) * u).astype(BF16)
        contrib = _dot(a, wd_ref[sl, :])
        if c == 0:
            acc_ref[...] = contrib
        else:
            acc_ref[...] += contrib
    o_ref[...] = x + 0.5 * acc_ref[...]


def _ffn_kernel(x_ref, nw_ref, wg_ref, wu_ref, wd_ref, o_ref, acc_ref):
    _ffn_body(x_ref[...], nw_ref, wg_ref, wu_ref, wd_ref, o_ref, acc_ref)


def _ffn_proj_kernel(x_ref, a_ref, wo_ref, nw_ref, wg_ref, wu_ref, wd_ref, o_ref, acc_ref):
    x = x_ref[...] + _dot(a_ref[...], wo_ref[...])
    _ffn_body(x, nw_ref, wg_ref, wu_ref, wd_ref, o_ref, acc_ref)


def _ffn(x, nw, wg, wu, wd, attn=None, wo=None):
    t = x.shape[0]
    rows = min(FFN_ROWS, t)
    row_spec = pl.BlockSpec((rows, D_MODEL), lambda i: (i, 0))
    w_specs = [_const_spec((1, D_MODEL)), _const_spec((D_MODEL, D_FF)),
               _const_spec((D_MODEL, D_FF)), _const_spec((D_FF, D_MODEL))]
    if attn is None:
        kern, ins, specs = _ffn_kernel, (x,), [row_spec]
    else:
        kern, ins = _ffn_proj_kernel, (x, attn, wo)
        specs = [row_spec, row_spec, _const_spec((D_MODEL, D_MODEL))]
    return pl.pallas_call(
        kern,
        grid=(t // rows,),
        in_specs=specs + w_specs,
        out_specs=row_spec,
        out_shape=jax.ShapeDtypeStruct((t, D_MODEL), F32),
        scratch_shapes=[pltpu.VMEM((rows, D_MODEL), F32)],
        compiler_params=_params(("parallel",)),
        name="ffn" if attn is None else "ffn_proj",
    )(*ins, nw, wg, wu, wd)


def _ssm_in_kernel(x_ref, nw_ref, wz_ref, wxbc_ref, wdt_ref, cw_ref, cb_ref, dtb_ref,
                   z_ref, xbc_ref, dt_ref, ubuf):
    rows = x_ref.shape[0]
    h = _rms(x_ref[...], nw_ref[...]).astype(BF16)
    z_ref[...] = _dot(h, wz_ref[...]).astype(BF16)
    dtv = _dot(h, wdt_ref[...]) + dtb_ref[...]
    dt_ref[...] = jnp.maximum(dtv, 0.0) + jnp.log1p(jnp.exp(-jnp.abs(dtv)))

    @pl.when(pl.program_id(1) == 0)
    def _():
        ubuf[0:SUBLANES, :] = jnp.zeros((SUBLANES, CONV_DIM), F32)

    for c in range(CONV_DIM // SSM_IN_COLS):
        sl = slice(c * SSM_IN_COLS, (c + 1) * SSM_IN_COLS)
        ubuf[SUBLANES:SUBLANES + rows, sl] = _dot(h, wxbc_ref[:, sl])
        acc = jnp.broadcast_to(cb_ref[:, sl], (rows, SSM_IN_COLS))
        for k in range(CONV_K):
            off = SUBLANES - (CONV_K - 1) + k
            acc = acc + cw_ref[k:k + 1, sl] * ubuf[off:off + rows, sl]
        xbc_ref[:, sl] = _silu(acc).astype(BF16)
    ubuf[0:SUBLANES, :] = ubuf[rows:rows + SUBLANES, :]


def _ssm_in(x, bsz, seq, nw, wz, wxbc, wdt, cw, cb, dtb):
    t = x.shape[0]
    rows = min(SSM_IN_ROWS, seq)
    ns = seq // rows
    row = lambda w: pl.BlockSpec((rows, w), lambda b, s: (b * ns + s, 0))
    return pl.pallas_call(
        _ssm_in_kernel,
        grid=(bsz, ns),
        in_specs=[row(D_MODEL), _const_spec((1, D_MODEL)), _const_spec((D_MODEL, D_INNER)),
                  _const_spec((D_MODEL, CONV_DIM)), _const_spec((D_MODEL, LANES)),
                  _const_spec((CONV_K, CONV_DIM)), _const_spec((1, CONV_DIM)),
                  _const_spec((1, LANES))],
        out_specs=[row(D_INNER), row(CONV_DIM), row(LANES)],
        out_shape=[jax.ShapeDtypeStruct((t, D_INNER), BF16),
                   jax.ShapeDtypeStruct((t, CONV_DIM), BF16),
                   jax.ShapeDtypeStruct((t, LANES), F32)],
        scratch_shapes=[pltpu.VMEM((rows + SUBLANES, CONV_DIM), F32)],
        compiler_params=_params(("parallel", "arbitrary")),
        name="ssm_in",
    )(x, nw, wz, wxbc, wdt, cw, cb, dtb)


def _ssd_kernel(x_ref, z_ref, xbc_ref, dt_ref, a_ref, d_ref, nw_ref, tril_ref, exp_ref,
                wo_ref, o_ref, state, ybuf):
    L = CHUNK

    @pl.when(pl.program_id(1) == 0)
    def _():
        state[...] = jnp.zeros(state.shape, F32)

    dt = dt_ref[...]
    dta = dt * a_ref[...]
    hi, mid, lo = _split3(dta)
    cs = _dot(tril_ref[...], jnp.concatenate([hi, mid, lo], axis=1))
    acs = cs[:, :LANES] + cs[:, LANES:2 * LANES] + cs[:, 2 * LANES:]
    acs_last = acs[L - 1:L, :]
    eacs = jnp.exp(acs)
    wgt = dt * jnp.exp(acs_last - acs)
    acs_t = acs.T
    dt_t = dt.T

    w_hi, w_lo = _split2(wgt)
    e_hi, e_lo = _split2(eacs)
    stacked = jnp.concatenate([w_hi, w_lo, e_hi, e_lo], axis=0)

    row = lax.broadcasted_iota(jnp.int32, (L, L), 0)
    col = lax.broadcasted_iota(jnp.int32, (L, L), 1)
    causal = col <= row

    for g in range(SSM_GROUPS):
        gsl = slice(g * GROUP_COLS, (g + 1) * GROUP_COLS)
        ex = _dot(stacked, exp_ref[:, gsl])
        wgt_x = ex[0:L] + ex[L:2 * L]
        eacs_x = ex[2 * L:3 * L] + ex[3 * L:4 * L]
        xs_b = xbc_ref[:, gsl]
        xs = xs_b.astype(F32)
        bm = xbc_ref[:, D_INNER + g * SSM_STATE:D_INNER + (g + 1) * SSM_STATE]
        cm = xbc_ref[:, D_INNER + SSM_GROUPS * SSM_STATE + g * SSM_STATE:
                     D_INNER + SSM_GROUPS * SSM_STATE + (g + 1) * SSM_STATE]
        cb = _dot_nt(cm, bm)
        st = state[g]
        y_off = _dot(cm, st.astype(BF16)) * eacs_x
        state[g] = st * eacs_x[L - 1:L, :] + _dot_tn(bm, (xs * wgt_x).astype(BF16))
        y_heads = []
        for r in range(SSM_HPG):
            hd = g * SSM_HPG + r
            seg = acs[:, hd:hd + 1] - acs_t[hd:hd + 1, :]
            decay = jnp.exp(jnp.where(causal, seg, -jnp.inf))
            m = (cb * decay * dt_t[hd:hd + 1, :]).astype(BF16)
            y_heads.append(_dot(m, xs_b[:, r * SSM_HEAD_DIM:(r + 1) * SSM_HEAD_DIM]))
        y = jnp.concatenate(y_heads, axis=1) + y_off + d_ref[:, gsl] * xs
        gt = y * _silu(z_ref[:, gsl].astype(F32))
        gt = gt * lax.rsqrt(jnp.mean(gt * gt, axis=-1, keepdims=True) + EPS)
        ybuf[:, gsl] = (gt * nw_ref[:, gsl]).astype(BF16)
    o_ref[...] = x_ref[...] + _dot(ybuf[...], wo_ref[...])


def _ssd(x, bsz, seq, z, xbc, dt, a_row, d_row, nw, tril, expand, wo):
    t = x.shape[0]
    nc = seq // CHUNK
    row = lambda w: pl.BlockSpec((CHUNK, w), lambda b, c: (b * nc + c, 0))
    return pl.pallas_call(
        _ssd_kernel,
        grid=(bsz, nc),
        in_specs=[row(D_MODEL), row(D_INNER), row(CONV_DIM), row(LANES),
                  _const_spec((1, LANES)), _const_spec((1, D_INNER)), _const_spec((1, D_INNER)),
                  _const_spec((CHUNK, CHUNK)), _const_spec((LANES, D_INNER)),
                  _const_spec((D_INNER, D_MODEL))],
        out_specs=row(D_MODEL),
        out_shape=jax.ShapeDtypeStruct((t, D_MODEL), F32),
        scratch_shapes=[pltpu.VMEM((SSM_GROUPS, SSM_STATE, GROUP_COLS), F32),
                        pltpu.VMEM((CHUNK, D_INNER), BF16)],
        compiler_params=_params(("parallel", "arbitrary")),
        name="ssd",
    )(x, z, xbc, dt, a_row, d_row, nw, tril, expand, wo)


def _group_rstd(sq, group):
    lane = lax.broadcasted_iota(jnp.int32, sq.shape, 1)
    res = jnp.zeros_like(sq)
    for gi in range(LANES // group):
        msk = (lane >= gi * group) & (lane < (gi + 1) * group)
        s = jnp.sum(jnp.where(msk, sq, 0.0), axis=-1, keepdims=True)
        res = jnp.where(msk, lax.rsqrt(s * (1.0 / group) + EPS), res)
    return res


def _mla_in_kernel(x_ref, pos_ref, nw_ref, wcq_ref, wckv_ref, wkr_ref, wkrs_ref,
                   qan_ref, kvan_ref, wqn_ref, wqr_ref, wqrs_ref, wkn_ref, wvt_ref,
                   gqn_ref, gkn_ref, gqr_ref, gqrs_ref, gkr_ref, gkrs_ref, freq_ref, sgn_ref,
                   qn_ref, qr_ref, kn_ref, kr_ref, vt_ref):
    h = _rms(x_ref[...], nw_ref[...]).astype(BF16)
    cq = _rms(_dot(h, wcq_ref[...]), qan_ref[...]).astype(BF16)
    ckv = _rms(_dot(h, wckv_ref[...]), kvan_ref[...]).astype(BF16)

    ang = pos_ref[...].astype(F32) * freq_ref[...]
    cos = jnp.cos(ang)
    sin = jnp.sin(ang) * sgn_ref[...]

    qscale = QK_DIM ** -0.5 * math.log2(math.e)

    vt_ref[...] = _dot_nt(wvt_ref[...], ckv).astype(BF16)
    for c in range(MLA_HEADS * QK_NOPE // LANES):
        sl = slice(c * LANES, (c + 1) * LANES)
        qn = _dot(cq, wqn_ref[:, sl])
        qn_ref[:, sl] = (qn * _group_rstd(qn * qn, QK_NOPE) * (gqn_ref[...] * qscale)).astype(BF16)
        kn = _dot(ckv, wkn_ref[:, sl])
        kn_ref[:, sl] = (kn * _group_rstd(kn * kn, QK_NOPE) * gkn_ref[...]).astype(BF16)
    for c in range(MLA_HEADS * QK_ROPE // LANES):
        sl = slice(c * LANES, (c + 1) * LANES)
        qr = _dot(cq, wqr_ref[:, sl])
        qrs = _dot(cq, wqrs_ref[:, sl])
        rot = qr * gqr_ref[...] * cos + qrs * gqrs_ref[...] * sin
        qr_ref[:, sl] = (rot * _group_rstd(qr * qr, QK_ROPE) * qscale).astype(BF16)
    kr = _dot(h, wkr_ref[...])
    krs = _dot(h, wkrs_ref[...])
    rot = kr * gkr_ref[...] * cos + krs * gkrs_ref[...] * sin
    kr_ref[...] = (rot * _group_rstd(kr * kr, QK_ROPE)).astype(BF16)


def _mla_in(x, pos, bsz, seq, ws):
    t = x.shape[0]
    rows = min(MLA_IN_ROWS, seq)
    ns = seq // rows
    row = lambda w: pl.BlockSpec((rows, w), lambda i: (i, 0))
    nope = MLA_HEADS * QK_NOPE
    rope = MLA_HEADS * QK_ROPE
    vdim = MLA_HEADS * V_DIM
    shapes = [(1, D_MODEL), (D_MODEL, Q_LORA), (D_MODEL, KV_LORA), (D_MODEL, LANES), (D_MODEL, LANES),
              (1, Q_LORA), (1, KV_LORA), (Q_LORA, nope), (Q_LORA, rope), (Q_LORA, rope),
              (KV_LORA, nope), (vdim, KV_LORA)] + [(1, LANES)] * 8
    vt_spec = pl.BlockSpec((None, vdim, rows), lambda i: (i // ns, 0, i % ns))
    return pl.pallas_call(
        _mla_in_kernel,
        grid=(t // rows,),
        in_specs=[row(D_MODEL), row(1)] + [_const_spec(s) for s in shapes],
        out_specs=[row(nope), row(rope), row(nope), row(LANES), vt_spec],
        out_shape=[jax.ShapeDtypeStruct((t, nope), BF16), jax.ShapeDtypeStruct((t, rope), BF16),
                   jax.ShapeDtypeStruct((t, nope), BF16), jax.ShapeDtypeStruct((t, LANES), BF16),
                   jax.ShapeDtypeStruct((bsz, vdim, seq), BF16)],
        compiler_params=_params(("parallel",)),
        name="mla_in",
    )(x, pos, *ws)


def _attn_kernel(qn_ref, qr_ref, kn_ref, kr_ref, vt_ref, o_ref, m_ref, l_ref, acc_ref):
    blk = qn_ref.shape[0]
    p = pl.program_id(1)
    i = pl.program_id(2)
    lane = lax.broadcasted_iota(jnp.int32, (blk, LANES), 1)
    qn = qn_ref[...]
    qr = qr_ref[...]
    zero = jnp.zeros_like(qn)
    rope_lo = (p % 2) * (2 * QK_ROPE)
    qs = []
    for hh in range(2):
        nope_part = jnp.where((lane >= hh * QK_NOPE) & (lane < (hh + 1) * QK_NOPE), qn, zero)
        lo = rope_lo + hh * QK_ROPE
        rope_part = jnp.where((lane >= lo) & (lane < lo + QK_ROPE), qr, zero)
        qs.append(jnp.concatenate([nope_part, rope_part], axis=1))

    m_ref[...] = jnp.full(m_ref.shape, -jnp.inf, F32)
    l_ref[...] = jnp.zeros(l_ref.shape, F32)
    acc_ref[...] = jnp.zeros(acc_ref.shape, F32)

    chains = [(hh, q0) for hh in range(2) for q0 in range(0, blk, ATTN_QSUB)]

    def process(blocks):
        starts = [pl.multiple_of(j * blk, blk) for j, _ in blocks]
        scores = []
        for start in starts:
            kfull = jnp.concatenate([kn_ref[pl.ds(start, blk), :], kr_ref[pl.ds(start, blk), :]], axis=1)
            scores.append([_dot_nt(kfull, qs[hh][q0:q0 + ATTN_QSUB, :]) for hh, q0 in chains])
        for (_, masked), start, block_scores in zip(blocks, starts, scores):
            vt = vt_ref[:, pl.ds(start, blk)]
            for (hh, q0), s in zip(chains, block_scores):
                qsl = slice(q0, q0 + ATTN_QSUB)
                if masked:
                    key = lax.broadcasted_iota(jnp.int32, (blk, ATTN_QSUB), 0)
                    qry = lax.broadcasted_iota(jnp.int32, (blk, ATTN_QSUB), 1) + q0
                    s = jnp.where(key <= qry, s, -jnp.inf)
                m_old = m_ref[hh, :, qsl]
                m_new = jnp.maximum(m_old, jnp.max(s, axis=0, keepdims=True))
                alpha = jnp.exp2(m_old - m_new)
                pr = jnp.exp2(s - m_new)
                l_ref[hh, :, qsl] = alpha * l_ref[hh, :, qsl] + jnp.sum(pr, axis=0, keepdims=True)
                pv = _dot(vt, pr.astype(BF16))
                acc_ref[hh, :, qsl] = (alpha * acc_ref[hh, :, qsl]
                                       + pv[hh * V_DIM:(hh + 1) * V_DIM, :])
                m_ref[hh, :, qsl] = m_new

    def body(jj, carry):
        process([(2 * jj, False), (2 * jj + 1, False)])
        return carry

    lax.fori_loop(0, i // 2, body, 0)

    @pl.when(i % 2 == 1)
    def _():
        process([(i - 1, False), (i, True)])

    @pl.when(i % 2 == 0)
    def _():
        process([(i, True)])

    ot = jnp.concatenate([acc_ref[0] / l_ref[0], acc_ref[1] / l_ref[1]], axis=0)
    o_ref[...] = ot.T.astype(BF16)


def _attn(qn, qr, kn, kr, vt, bsz, seq):
    t = qn.shape[0]
    blk = min(ATTN_BLOCK, seq)
    nq = seq // blk
    npair = MLA_HEADS // 2
    qspec = pl.BlockSpec((blk, LANES), lambda b, p, i: (b * nq + i, p))
    qrspec = pl.BlockSpec((blk, LANES), lambda b, p, i: (b * nq + i, p // 2))
    kspec = pl.BlockSpec((seq, LANES), lambda b, p, i: (b, p))
    krspec = pl.BlockSpec((seq, LANES), lambda b, p, i: (b, 0))
    vtspec = pl.BlockSpec((None, 2 * V_DIM, seq), lambda b, p, i: (b, p, 0))
    return pl.pallas_call(
        _attn_kernel,
        grid=(bsz, npair, nq),
        in_specs=[qspec, qrspec, kspec, krspec, vtspec],
        out_specs=qspec,
        out_shape=jax.ShapeDtypeStruct((t, MLA_HEADS * V_DIM), BF16),
        scratch_shapes=[pltpu.VMEM((2, 1, blk), F32), pltpu.VMEM((2, 1, blk), F32),
                        pltpu.VMEM((2, V_DIM, blk), F32)],
        compiler_params=_params(("parallel", "parallel", "arbitrary")),
        name="attn",
    )(qn, qr, kn, kr, vt)


def _row(v, width=None):
    v = v.astype(F32).reshape(1, -1)
    if width is not None and v.shape[1] < width:
        v = jnp.pad(v, ((0, 0), (0, width - v.shape[1])))
    return v


def _swap_halves(w):
    k, n = w.shape
    w = w.reshape(k, n // QK_ROPE, 2, QK_ROPE // 2)
    return w[:, :, ::-1, :].reshape(k, n)


def kernel(x, positions, norm_w, ffn_w_gate, ffn_w_up, ffn_w_down, ssm_w_in, ssm_conv_w, ssm_conv_b, ssm_dt_bias, ssm_a_log, ssm_d, ssm_norm_w, ssm_w_out, mla_w_in, mla_q_a_norm, mla_kv_a_norm, mla_w_q_b, mla_w_kv_b, mla_q_norm, mla_k_norm, mla_w_out):
    bsz, seq, _ = x.shape
    t = bsz * seq
    depth = norm_w.shape[0]
    xf = x.reshape(t, D_MODEL)

    def ffn(xin, i, k, attn=None, wo=None):
        return _ffn(xin, _row(norm_w[i, 0 if k == 0 else 2]),
                    ffn_w_gate[i, k].astype(BF16), ffn_w_up[i, k].astype(BF16),
                    ffn_w_down[i, k].astype(BF16), attn, wo)

    for i in range(depth):
        xf = ffn(xf, i, 0)
        j = i // 2
        if i % 2 == 0:
            w_in = ssm_w_in[j]
            wz = w_in[:, :D_INNER].astype(BF16)
            wxbc = w_in[:, D_INNER:D_INNER + CONV_DIM].astype(BF16)
            wdt = jnp.pad(w_in[:, D_INNER + CONV_DIM:], ((0, 0), (0, LANES - SSM_HEADS))).astype(BF16)
            z, xbc, dt = _ssm_in(xf, bsz, seq, _row(norm_w[i, 1]), wz, wxbc, wdt,
                                 ssm_conv_w[j].astype(F32), _row(ssm_conv_b[j]),
                                 _row(ssm_dt_bias[j], LANES))
            a_row = _row(-jnp.exp(ssm_a_log[j].astype(F32)), LANES)
            d_row = _row(jnp.repeat(ssm_d[j].astype(F32), SSM_HEAD_DIM))
            tril = jnp.tril(jnp.ones((CHUNK, CHUNK), F32)).astype(BF16)
            head_of_col = jnp.arange(D_INNER, dtype=jnp.int32) // SSM_HEAD_DIM
            expand = (jnp.arange(LANES, dtype=jnp.int32)[:, None] == head_of_col[None, :]).astype(BF16)
            xf = _ssd(xf, bsz, seq, z, xbc, dt, a_row, d_row, _row(ssm_norm_w[j]), tril, expand,
                      ssm_w_out[j].astype(BF16))
            xf = ffn(xf, i, 1)
        else:
            w_in = mla_w_in[j]
            wcq = w_in[:, :Q_LORA].astype(BF16)
            wckv = w_in[:, Q_LORA:Q_LORA + KV_LORA].astype(BF16)
            wkr = jnp.tile(w_in[:, Q_LORA + KV_LORA:], (1, LANES // QK_ROPE))
            wq = mla_w_q_b[j].reshape(Q_LORA, MLA_HEADS, QK_DIM)
            wqn = wq[:, :, :QK_NOPE].reshape(Q_LORA, MLA_HEADS * QK_NOPE).astype(BF16)
            wqr = wq[:, :, QK_NOPE:].reshape(Q_LORA, MLA_HEADS * QK_ROPE)
            wkv = mla_w_kv_b[j].reshape(KV_LORA, MLA_HEADS, QK_NOPE + V_DIM)
            wkn = wkv[:, :, :QK_NOPE].reshape(KV_LORA, MLA_HEADS * QK_NOPE).astype(BF16)
            wvt = wkv[:, :, QK_NOPE:].reshape(KV_LORA, MLA_HEADS * V_DIM).T.astype(BF16)
            qg, kg = mla_q_norm[j].astype(F32), mla_k_norm[j].astype(F32)
            tile_n = lambda g: jnp.tile(g, LANES // g.shape[0]).reshape(1, LANES)
            swap = lambda g: jnp.concatenate([g[QK_ROPE // 2:], g[:QK_ROPE // 2]])
            inv_freq = 1.0 / (ROPE_THETA ** (jnp.arange(0, QK_ROPE, 2, dtype=F32) / QK_ROPE))
            freq = jnp.tile(inv_freq, LANES // (QK_ROPE // 2)).reshape(1, LANES)
            sgn = jnp.tile(jnp.concatenate([-jnp.ones(QK_ROPE // 2, F32), jnp.ones(QK_ROPE // 2, F32)]),
                           LANES // QK_ROPE).reshape(1, LANES)
            ws = [_row(norm_w[i, 1]), wcq, wckv, wkr.astype(BF16), _swap_halves(wkr).astype(BF16),
                  _row(mla_q_a_norm[j]), _row(mla_kv_a_norm[j]), wqn, wqr.astype(BF16),
                  _swap_halves(wqr).astype(BF16), wkn, wvt,
                  tile_n(qg[:QK_NOPE]), tile_n(kg[:QK_NOPE]), tile_n(qg[QK_NOPE:]),
                  tile_n(swap(qg[QK_NOPE:])), tile_n(kg[QK_NOPE:]), tile_n(swap(kg[QK_NOPE:])),
                  freq, sgn]
            qn, qr, kn, kr, vt = _mla_in(xf, positions.reshape(t, 1).astype(jnp.int32), bsz, seq, ws)
            o = _attn(qn, qr, kn, kr, vt, bsz, seq)
            xf = ffn(xf, i, 1, attn=o, wo=mla_w_out[j].astype(BF16))
    return xf.reshape(bsz, seq, D_MODEL)
```

```python
import functools
import math

import jax
import jax.numpy as jnp
from jax import lax
from jax.experimental import pallas as pl
from jax.experimental.pallas import tpu as pltpu

F32 = jnp.float32
BF16 = jnp.bfloat16

D_MODEL = 1024
D_FF = 2816
EPS = 1e-6

SSM_HEADS = 32
SSM_HEAD_DIM = 64
SSM_GROUPS = 8
SSM_HPG = SSM_HEADS // SSM_GROUPS
SSM_STATE = 128
D_INNER = SSM_HEADS * SSM_HEAD_DIM
GROUP_COLS = SSM_HPG * SSM_HEAD_DIM
CONV_K = 4
CONV_DIM = D_INNER + 2 * SSM_GROUPS * SSM_STATE
CHUNK = 128

MLA_HEADS = 16
Q_LORA = 384
KV_LORA = 256
QK_NOPE = 64
QK_ROPE = 32
V_DIM = 64
QK_DIM = QK_NOPE + QK_ROPE
ROPE_THETA = 10000.0

LANES = 128
SUBLANES = 8
PERM_ROWS = CHUNK // SUBLANES
VMEM_LIMIT = 56 * 1024 * 1024

FFN_ROWS = 512
FFN_COLS = 256
SSM_IN_ROWS = 256
SSM_IN_COLS = 256
SSD_ROWS = 2 * CHUNK
MLA_IN_ROWS = 256
ATTN_BLOCK = 512
ATTN_QSUB = 256
ATTN_GROUP = 4


def _dot(a, b):
    return jnp.dot(a, b, preferred_element_type=F32)


def _dot_nt(a, b):
    return lax.dot_general(a, b, (((1,), (1,)), ((), ())), preferred_element_type=F32)


def _dot_tn(a, b):
    return lax.dot_general(a, b, (((0,), (0,)), ((), ())), preferred_element_type=F32)


def _rms(x, w):
    return x * lax.rsqrt(jnp.mean(x * x, axis=-1, keepdims=True) + EPS) * w


def _silu(x):
    return x * jax.nn.sigmoid(x)


def _split2(x):
    hi = x.astype(BF16)
    lo = (x - hi.astype(F32)).astype(BF16)
    return hi, lo


def _split3(x):
    hi = x.astype(BF16)
    r = x - hi.astype(F32)
    mid = r.astype(BF16)
    lo = (r - mid.astype(F32)).astype(BF16)
    return hi, mid, lo


def _const_spec(shape):
    zeros = (0,) * len(shape)
    return pl.BlockSpec(shape, lambda *_: zeros, pipeline_mode=pl.Buffered(1))


def _params(sem):
    return pltpu.CompilerParams(dimension_semantics=sem, vmem_limit_bytes=VMEM_LIMIT)


def _ffn_body(x, nw_ref, wg_ref, wu_ref, wd_ref, o_ref, acc_ref):
    h = _rms(x, nw_ref[...]).astype(BF16)
    for c in range(D_FF // FFN_COLS):
        sl = slice(c * FFN_COLS, (c + 1) * FFN_COLS)
        g = _dot(h, wg_ref[:, sl])
        u = _dot(h, wu_ref[:, sl])
        a = (_silu(g) * u).astype(BF16)
        contrib = _dot(a, wd_ref[sl, :])
        if c == 0:
            acc_ref[...] = contrib
        else:
            acc_ref[...] += contrib
    o_ref[...] = x + 0.5 * acc_ref[...]


def _ffn_kernel(x_ref, nw_ref, wg_ref, wu_ref, wd_ref, o_ref, acc_ref):
    _ffn_body(x_ref[...], nw_ref, wg_ref, wu_ref, wd_ref, o_ref, acc_ref)


def _ffn_proj_kernel(x_ref, a_ref, wo_ref, nw_ref, wg_ref, wu_ref, wd_ref, o_ref, acc_ref):
    x = x_ref[...] + _dot(a_ref[...], wo_ref[...])
    _ffn_body(x, nw_ref, wg_ref, wu_ref, wd_ref, o_ref, acc_ref)


def _ffn(x, nw, wg, wu, wd, attn=None, wo=None):
    t = x.shape[0]
    rows = min(FFN_ROWS, t)
    row_spec = pl.BlockSpec((rows, D_MODEL), lambda i: (i, 0))
    w_specs = [_const_spec((1, D_MODEL)), _const_spec((D_MODEL, D_FF)),
               _const_spec((D_MODEL, D_FF)), _const_spec((D_FF, D_MODEL))]
    if attn is None:
        kern, ins, specs = _ffn_kernel, (x,), [row_spec]
    else:
        kern, ins = _ffn_proj_kernel, (x, attn, wo)
        specs = [row_spec, row_spec, _const_spec((D_MODEL, D_MODEL))]
    return pl.pallas_call(
        kern,
        grid=(t // rows,),
        in_specs=specs + w_specs,
        out_specs=row_spec,
        out_shape=jax.ShapeDtypeStruct((t, D_MODEL), F32),
        scratch_shapes=[pltpu.VMEM((rows, D_MODEL), F32)],
        compiler_params=_params(("parallel",)),
        name="ffn" if attn is None else "ffn_proj",
    )(*ins, nw, wg, wu, wd)


def _ssm_in_kernel(x_ref, nw_ref, perm_ref, wz_ref, wxbc_ref, wdt_ref, cw_ref, cb_ref, dtb_ref,
                   z_ref, xbc_ref, dt_ref, carry):
    rows = x_ref.shape[0]
    nchunk = rows // CHUNK
    ntail = CONV_K - 1
    h_tok = _rms(x_ref[...], nw_ref[...]).astype(BF16)
    h = jnp.concatenate([_dot(perm_ref[...], h_tok[ch * CHUNK:(ch + 1) * CHUNK, :]).astype(BF16)
                         for ch in range(nchunk)], axis=0)

    @pl.when(pl.program_id(1) == 0)
    def _():
        carry[...] = jnp.zeros(carry.shape, F32)

    sub = lax.broadcasted_iota(jnp.int32, (SUBLANES, SSM_IN_COLS), 0)
    n_conv = CONV_DIM // SSM_IN_COLS
    n_z = D_INNER // SSM_IN_COLS
    for c in range(n_conv):
        sl = slice(c * SSM_IN_COLS, (c + 1) * SSM_IN_COLS)
        u = _dot(h, wxbc_ref[:, sl])
        if c % (n_conv // n_z) == 0:
            zsl = slice(c // (n_conv // n_z) * SSM_IN_COLS, (c // (n_conv // n_z) + 1) * SSM_IN_COLS)
            z_ref[:, zsl] = _dot(h, wz_ref[:, zsl]).astype(BF16)
        vrow = [[u[(ch * PERM_ROWS + i) * SUBLANES:(ch * PERM_ROWS + i + 1) * SUBLANES, :]
                 for i in range(PERM_ROWS)] for ch in range(nchunk)]
        wrapped = []
        for ch in range(nchunk):
            wr = {}
            for k in range(PERM_ROWS - ntail, PERM_ROWS):
                kk = k - (PERM_ROWS - ntail)
                prev = carry[kk * SUBLANES:(kk + 1) * SUBLANES, sl] if ch == 0 else vrow[ch - 1][k]
                wr[k] = jnp.where(sub == 0, pltpu.roll(prev, 1, axis=0), pltpu.roll(vrow[ch][k], 1, axis=0))
            wrapped.append(wr)
        acc = cb_ref[:, sl] + cw_ref[CONV_K - 1:CONV_K, sl] * u
        for back in range(1, CONV_K):
            shifted = jnp.concatenate(
                [vrow[ch][i - back] if i >= back else wrapped[ch][PERM_ROWS + i - back]
                 for ch in range(nchunk) for i in range(PERM_ROWS)], axis=0)
            acc = acc + cw_ref[CONV_K - 1 - back:CONV_K - back, sl] * shifted
        xbc_ref[:, sl] = _silu(acc).astype(BF16)
        for kk in range(ntail):
            carry[kk * SUBLANES:(kk + 1) * SUBLANES, sl] = vrow[nchunk - 1][PERM_ROWS - ntail + kk]
    dtv = _dot(h, wdt_ref[...]) + dtb_ref[...]
    dt_ref[...] = jnp.maximum(dtv, 0.0) + jnp.log1p(jnp.exp(-jnp.abs(dtv)))


def _ssm_in(x, bsz, seq, nw, perm, wz, wxbc, wdt, cw, cb, dtb):
    t = x.shape[0]
    rows = min(SSM_IN_ROWS, seq)
    ns = seq // rows
    row = lambda w: pl.BlockSpec((rows, w), lambda b, s: (b * ns + s, 0))
    return pl.pallas_call(
        _ssm_in_kernel,
        grid=(bsz, ns),
        in_specs=[row(D_MODEL), _const_spec((1, D_MODEL)), _const_spec((CHUNK, CHUNK)),
                  _const_spec((D_MODEL, D_INNER)),
                  _const_spec((D_MODEL, CONV_DIM)), _const_spec((D_MODEL, LANES)),
                  _const_spec((CONV_K, CONV_DIM)), _const_spec((1, CONV_DIM)),
                  _const_spec((1, LANES))],
        out_specs=[row(D_INNER), row(CONV_DIM), row(LANES)],
        out_shape=[jax.ShapeDtypeStruct((t, D_INNER), BF16),
                   jax.ShapeDtypeStruct((t, CONV_DIM), BF16),
                   jax.ShapeDtypeStruct((t, LANES), F32)],
        scratch_shapes=[pltpu.VMEM(((CONV_K - 1) * SUBLANES, CONV_DIM), F32)],
        compiler_params=_params(("parallel", "arbitrary")),
        name="ssm_in",
    )(x, nw, perm, wz, wxbc, wdt, cw, cb, dtb)


def _ssd_kernel(x_ref, z_ref, xbc_ref, dt_ref, a_ref, d_ref, nw_ref, tril_ref, unperm_ref, exp_ref,
                wo_ref, o_ref, state, ybuf):
    L = CHUNK

    @pl.when(pl.program_id(1) == 0)
    def _():
        state[...] = jnp.zeros(state.shape, F32)

    row = lax.broadcasted_iota(jnp.int32, (L, L), 0)
    col = lax.broadcasted_iota(jnp.int32, (L, L), 1)
    tok = lambda r: (r % SUBLANES) * PERM_ROWS + r // SUBLANES
    causal = tok(col) <= tok(row)

    for ck in range(x_ref.shape[0] // L):
        rsl = slice(ck * L, (ck + 1) * L)
        dt = dt_ref[rsl, :]
        dta = dt * a_ref[...]
        hi, mid, lo = _split3(dta)
        cs = _dot(tril_ref[...], jnp.concatenate([hi, mid, lo], axis=1))
        acs = cs[:, :LANES] + cs[:, LANES:2 * LANES] + cs[:, 2 * LANES:]
        acs_last = acs[L - 1:L, :]
        eacs = jnp.exp(acs)
        wgt = dt * jnp.exp(acs_last - acs)
        acs_t = acs.T
        dt_t = dt.T

        wgt_s = jnp.concatenate(_split2(wgt), axis=1)
        eacs_s = jnp.concatenate(_split2(eacs), axis=1)

        def group_front(g):
            gsl = slice(g * GROUP_COLS, (g + 1) * GROUP_COLS)
            wgt_x = _dot(wgt_s, exp_ref[:, gsl])
            eacs_x = _dot(eacs_s, exp_ref[:, gsl])
            xs_b = xbc_ref[rsl, gsl]
            xs = xs_b.astype(F32)
            bm = xbc_ref[rsl, D_INNER + g * SSM_STATE:D_INNER + (g + 1) * SSM_STATE]
            cm = xbc_ref[rsl, D_INNER + SSM_GROUPS * SSM_STATE + g * SSM_STATE:
                         D_INNER + SSM_GROUPS * SSM_STATE + (g + 1) * SSM_STATE]
            cb = _dot_nt(cm, bm)
            st = state[g]
            y_off = _dot(cm, st.astype(BF16)) * eacs_x
            state[g] = st * eacs_x[L - 1:L, :] + _dot_tn(bm, (xs * wgt_x).astype(BF16))
            return cb, y_off, xs_b, xs

        def group_back(g, cb, y_off, xs_b, xs):
            gsl = slice(g * GROUP_COLS, (g + 1) * GROUP_COLS)
            y_heads = []
            for r in range(SSM_HPG):
                hd = g * SSM_HPG + r
                seg = acs[:, hd:hd + 1] - acs_t[hd:hd + 1, :]
                decay = jnp.exp(jnp.where(causal, seg, -jnp.inf))
                m = (cb * decay * dt_t[hd:hd + 1, :]).astype(BF16)
                y_heads.append(_dot(m, xs_b[:, r * SSM_HEAD_DIM:(r + 1) * SSM_HEAD_DIM]))
            y = jnp.concatenate(y_heads, axis=1) + y_off + d_ref[:, gsl] * xs
            gt = y * _silu(z_ref[rsl, gsl].astype(F32))
            gt = gt * lax.rsqrt(jnp.mean(gt * gt, axis=-1, keepdims=True) + EPS)
            ybuf[rsl, gsl] = (gt * nw_ref[:, gsl]).astype(BF16)

        pending = group_front(0)
        for g in range(SSM_GROUPS):
            nxt = group_front(g + 1) if g + 1 < SSM_GROUPS else None
            group_back(g, *pending)
            pending = nxt
        ybuf[rsl, :] = _dot(unperm_ref[...], ybuf[rsl, :]).astype(BF16)
    o_ref[...] = x_ref[...] + _dot(ybuf[...], wo_ref[...])


def _ssd(x, bsz, seq, z, xbc, dt, a_row, d_row, nw, tril, unperm, expand, wo):
    t = x.shape[0]
    rows = min(SSD_ROWS, seq)
    nc = seq // rows
    row = lambda w: pl.BlockSpec((rows, w), lambda b, c: (b * nc + c, 0))
    return pl.pallas_call(
        _ssd_kernel,
        grid=(bsz, nc),
        in_specs=[row(D_MODEL), row(D_INNER), row(CONV_DIM), row(LANES),
                  _const_spec((1, LANES)), _const_spec((1, D_INNER)), _const_spec((1, D_INNER)),
                  _const_spec((CHUNK, CHUNK)), _const_spec((CHUNK, CHUNK)), _const_spec((2 * LANES, D_INNER)),
                  _const_spec((D_INNER, D_MODEL))],
        out_specs=row(D_MODEL),
        out_shape=jax.ShapeDtypeStruct((t, D_MODEL), F32),
        scratch_shapes=[pltpu.VMEM((SSM_GROUPS, SSM_STATE, GROUP_COLS), F32),
                        pltpu.VMEM((rows, D_INNER), BF16)],
        compiler_params=_params(("parallel", "arbitrary")),
        name="ssd",
    )(x, z, xbc, dt, a_row, d_row, nw, tril, unperm, expand, wo)


def _group_rstd(sq, group):
    lane = lax.broadcasted_iota(jnp.int32, sq.shape, 1)
    res = jnp.zeros_like(sq)
    for gi in range(LANES // group):
        msk = (lane >= gi * group) & (lane < (gi + 1) * group)
        s = jnp.sum(jnp.where(msk, sq, 0.0), axis=-1, keepdims=True)
        res = jnp.where(msk, lax.rsqrt(s * (1.0 / group) + EPS), res)
    return res


def _mla_in_kernel(x_ref, pos_ref, nw_ref, wcq_ref, wckv_ref, wkr_ref, wkrs_ref,
                   qan_ref, kvan_ref, wqn_ref, wqr_ref, wqrs_ref, wkn_ref, wvt_ref,
                   gqn_ref, gkn_ref, gqr_ref, gqrs_ref, gkr_ref, gkrs_ref, freq_ref, sgn_ref,
                   qn_ref, qr_ref, kn_ref, kr_ref, vt_ref):
    h = _rms(x_ref[...], nw_ref[...]).astype(BF16)
    cq = _rms(_dot(h, wcq_ref[...]), qan_ref[...]).astype(BF16)
    ckv = _rms(_dot(h, wckv_ref[...]), kvan_ref[...]).astype(BF16)

    ang = pos_ref[...].astype(F32) * freq_ref[...]
    cos = jnp.cos(ang)
    sin = jnp.sin(ang) * sgn_ref[...]

    qscale = QK_DIM ** -0.5 * math.log2(math.e)

    vt_ref[...] = _dot_nt(wvt_ref[...], ckv).astype(BF16)
    for c in range(MLA_HEADS * QK_NOPE // LANES):
        sl = slice(c * LANES, (c + 1) * LANES)
        qn = _dot(cq, wqn_ref[:, sl])
        qn_ref[:, sl] = (qn * _group_rstd(qn * qn, QK_NOPE) * (gqn_ref[...] * qscale)).astype(BF16)
        kn = _dot(ckv, wkn_ref[:, sl])
        kn_ref[:, sl] = (kn * _group_rstd(kn * kn, QK_NOPE) * gkn_ref[...]).astype(BF16)
    for c in range(MLA_HEADS * QK_ROPE // LANES):
        sl = slice(c * LANES, (c + 1) * LANES)
        qr = _dot(cq, wqr_ref[:, sl])
        qrs = _dot(cq, wqrs_ref[:, sl])
        rot = qr * gqr_ref[...] * cos + qrs * gqrs_ref[...] * sin
        qr_ref[:, sl] = (rot * _group_rstd(qr * qr, QK_ROPE) * qscale).astype(BF16)
    kr = _dot(h, wkr_ref[...])
    krs = _dot(h, wkrs_ref[...])
    rot = kr * gkr_ref[...] * cos + krs * gkrs_ref[...] * sin
    kr_ref[...] = (rot * _group_rstd(kr * kr, QK_ROPE)).astype(BF16)


def _mla_in(x, pos, bsz, seq, ws):
    t = x.shape[0]
    rows = min(MLA_IN_ROWS, seq)
    ns = seq // rows
    row = lambda w: pl.BlockSpec((rows, w), lambda i: (i, 0))
    nope = MLA_HEADS * QK_NOPE
    rope = MLA_HEADS * QK_ROPE
    vdim = MLA_HEADS * V_DIM
    shapes = [(1, D_MODEL), (D_MODEL, Q_LORA), (D_MODEL, KV_LORA), (D_MODEL, LANES), (D_MODEL, LANES),
              (1, Q_LORA), (1, KV_LORA), (Q_LORA, nope), (Q_LORA, rope), (Q_LORA, rope),
              (KV_LORA, nope), (vdim, KV_LORA)] + [(1, LANES)] * 8
    vt_spec = pl.BlockSpec((None, vdim, rows), lambda i: (i // ns, 0, i % ns))
    return pl.pallas_call(
        _mla_in_kernel,
        grid=(t // rows,),
        in_specs=[row(D_MODEL), row(1)] + [_const_spec(s) for s in shapes],
        out_specs=[row(nope), row(rope), row(nope), row(LANES), vt_spec],
        out_shape=[jax.ShapeDtypeStruct((t, nope), BF16), jax.ShapeDtypeStruct((t, rope), BF16),
                   jax.ShapeDtypeStruct((t, nope), BF16), jax.ShapeDtypeStruct((t, LANES), BF16),
                   jax.ShapeDtypeStruct((bsz, vdim, seq), BF16)],
        compiler_params=_params(("parallel",)),
        name="mla_in",
    )(x, pos, *ws)


def _attn_kernel(qn_ref, qr_ref, kn_ref, kr_ref, vt_ref, o_ref, m_ref, l_ref, acc_ref):
    blk = qn_ref.shape[0]
    p = pl.program_id(1)
    i = pl.program_id(2)
    lane = lax.broadcasted_iota(jnp.int32, (blk, LANES), 1)
    qn = qn_ref[...]
    qr = qr_ref[...]
    zero = jnp.zeros_like(qn)
    rope_lo = (p % 2) * (2 * QK_ROPE)
    qs = []
    for hh in range(2):
        nope_part = jnp.where((lane >= hh * QK_NOPE) & (lane < (hh + 1) * QK_NOPE), qn, zero)
        lo = rope_lo + hh * QK_ROPE
        rope_part = jnp.where((lane >= lo) & (lane < lo + QK_ROPE), qr, zero)
        qs.append(jnp.concatenate([nope_part, rope_part], axis=1))

    m_ref[...] = jnp.full(m_ref.shape, -jnp.inf, F32)
    l_ref[...] = jnp.zeros(l_ref.shape, F32)
    acc_ref[...] = jnp.zeros(acc_ref.shape, F32)

    chains = [(hh, q0) for hh in range(2) for q0 in range(0, blk, ATTN_QSUB)]

    def process(blocks):
        starts = [pl.multiple_of(j * blk, blk) for j, _ in blocks]
        scores = []
        for start in starts:
            kfull = jnp.concatenate([kn_ref[pl.ds(start, blk), :], kr_ref[pl.ds(start, blk), :]], axis=1)
            scores.append([_dot_nt(kfull, qs[hh][q0:q0 + ATTN_QSUB, :]) for hh, q0 in chains])
        for (_, masked), start, block_scores in zip(blocks, starts, scores):
            vt = vt_ref[:, pl.ds(start, blk)]
            for (hh, q0), s in zip(chains, block_scores):
                qsl = slice(q0, q0 + ATTN_QSUB)
                if masked:
                    key = lax.broadcasted_iota(jnp.int32, (blk, ATTN_QSUB), 0)
                    qry = lax.broadcasted_iota(jnp.int32, (blk, ATTN_QSUB), 1) + q0
                    s = jnp.where(key <= qry, s, -jnp.inf)
                m_old = m_ref[hh, :, qsl]
                m_new = jnp.maximum(m_old, jnp.max(s, axis=0, keepdims=True))
                alpha = jnp.exp2(m_old - m_new)
                pr = jnp.exp2(s - m_new)
                l_ref[hh, :, qsl] = alpha * l_ref[hh, :, qsl] + jnp.sum(pr, axis=0, keepdims=True)
                pv = _dot(vt, pr.astype(BF16))
                acc_ref[hh, :, qsl] = (alpha * acc_ref[hh, :, qsl]
                                       + pv[hh * V_DIM:(hh + 1) * V_DIM, :])
                m_ref[hh, :, qsl] = m_new

    grp = ATTN_GROUP
    n_full = i // grp

    def body(jj, carry):
        process([(grp * jj + k, False) for k in range(grp)])
        return carry

    lax.fori_loop(0, n_full, body, 0)

    for rem in range(grp):
        @pl.when(i - grp * n_full == rem)
        def _():
            process([(i - rem + k, False) for k in range(rem)] + [(i, True)])

    ot = jnp.concatenate([acc_ref[0] / l_ref[0], acc_ref[1] / l_ref[1]], axis=0)
    o_ref[...] = ot.T.astype(BF16)


def _attn(qn, qr, kn, kr, vt, bsz, seq):
    t = qn.shape[0]
    blk = min(ATTN_BLOCK, seq)
    nq = seq // blk
    npair = MLA_HEADS // 2
    qspec = pl.BlockSpec((blk, LANES), lambda b, p, i: (b * nq + i, p))
    qrspec = pl.BlockSpec((blk, LANES), lambda b, p, i: (b * nq + i, p // 2))
    kspec = pl.BlockSpec((seq, LANES), lambda b, p, i: (b, p))
    krspec = pl.BlockSpec((seq, LANES), lambda b, p, i: (b, 0))
    vtspec = pl.BlockSpec((None, 2 * V_DIM, seq), lambda b, p, i: (b, p, 0))
    return pl.pallas_call(
        _attn_kernel,
        grid=(bsz, npair, nq),
        in_specs=[qspec, qrspec, kspec, krspec, vtspec],
        out_specs=qspec,
        out_shape=jax.ShapeDtypeStruct((t, MLA_HEADS * V_DIM), BF16),
        scratch_shapes=[pltpu.VMEM((2, 1, blk), F32), pltpu.VMEM((2, 1, blk), F32),
                        pltpu.VMEM((2, V_DIM, blk), F32)],
        compiler_params=_params(("parallel", "parallel", "arbitrary")),
        name="attn",
    )(qn, qr, kn, kr, vt)


def _row(v, width=None):
    v = v.astype(F32).reshape(1, -1)
    if width is not None and v.shape[1] < width:
        v = jnp.pad(v, ((0, 0), (0, width - v.shape[1])))
    return v


def _swap_halves(w):
    k, n = w.shape
    w = w.reshape(k, n // QK_ROPE, 2, QK_ROPE // 2)
    return w[:, :, ::-1, :].reshape(k, n)


def kernel(x, positions, norm_w, ffn_w_gate, ffn_w_up, ffn_w_down, ssm_w_in, ssm_conv_w, ssm_conv_b, ssm_dt_bias, ssm_a_log, ssm_d, ssm_norm_w, ssm_w_out, mla_w_in, mla_q_a_norm, mla_kv_a_norm, mla_w_q_b, mla_w_kv_b, mla_q_norm, mla_k_norm, mla_w_out):
    bsz, seq, _ = x.shape
    t = bsz * seq
    depth = norm_w.shape[0]
    xf = x.reshape(t, D_MODEL)

    def ffn(xin, i, k, attn=None, wo=None):
        return _ffn(xin, _row(norm_w[i, 0 if k == 0 else 2]),
                    ffn_w_gate[i, k].astype(BF16), ffn_w_up[i, k].astype(BF16),
                    ffn_w_down[i, k].astype(BF16), attn, wo)

    for i in range(depth):
        xf = ffn(xf, i, 0)
        j = i // 2
        if i % 2 == 0:
            w_in = ssm_w_in[j]
            wz = w_in[:, :D_INNER].astype(BF16)
            wxbc = w_in[:, D_INNER:D_INNER + CONV_DIM].astype(BF16)
            wdt = jnp.pad(w_in[:, D_INNER + CONV_DIM:], ((0, 0), (0, LANES - SSM_HEADS))).astype(BF16)
            ridx = jnp.arange(CHUNK, dtype=jnp.int32)
            tok = (ridx % SUBLANES) * PERM_ROWS + ridx // SUBLANES
            perm = (tok[:, None] == ridx[None, :]).astype(BF16)
            z, xbc, dt = _ssm_in(xf, bsz, seq, _row(norm_w[i, 1]), perm, wz, wxbc, wdt,
                                 ssm_conv_w[j].astype(F32), _row(ssm_conv_b[j]),
                                 _row(ssm_dt_bias[j], LANES))
            a_row = _row(-jnp.exp(ssm_a_log[j].astype(F32)), LANES)
            d_row = _row(jnp.repeat(ssm_d[j].astype(F32), SSM_HEAD_DIM))
            tril = (tok[None, :] <= tok[:, None]).astype(BF16)
            head_of_col = jnp.arange(D_INNER, dtype=jnp.int32) // SSM_HEAD_DIM
            expand = (jnp.arange(2 * LANES, dtype=jnp.int32)[:, None] % LANES
                      == head_of_col[None, :]).astype(BF16)
            xf = _ssd(xf, bsz, seq, z, xbc, dt, a_row, d_row, _row(ssm_norm_w[j]), tril, perm.T, expand,
                      ssm_w_out[j].astype(BF16))
            xf = ffn(xf, i, 1)
        else:
            w_in = mla_w_in[j]
            wcq = w_in[:, :Q_LORA].astype(BF16)
            wckv = w_in[:, Q_LORA:Q_LORA + KV_LORA].astype(BF16)
            wkr = jnp.tile(w_in[:, Q_LORA + KV_LORA:], (1, LANES // QK_ROPE))
            wq = mla_w_q_b[j].reshape(Q_LORA, MLA_HEADS, QK_DIM)
            wqn = wq[:, :, :QK_NOPE].reshape(Q_LORA, MLA_HEADS * QK_NOPE).astype(BF16)
            wqr = wq[:, :, QK_NOPE:].reshape(Q_LORA, MLA_HEADS * QK_ROPE)
            wkv = mla_w_kv_b[j].reshape(KV_LORA, MLA_HEADS, QK_NOPE + V_DIM)
            wkn = wkv[:, :, :QK_NOPE].reshape(KV_LORA, MLA_HEADS * QK_NOPE).astype(BF16)
            wvt = wkv[:, :, QK_NOPE:].reshape(KV_LORA, MLA_HEADS * V_DIM).T.astype(BF16)
            qg, kg = mla_q_norm[j].astype(F32), mla_k_norm[j].astype(F32)
            tile_n = lambda g: jnp.tile(g, LANES // g.shape[0]).reshape(1, LANES)
            swap = lambda g: jnp.concatenate([g[QK_ROPE // 2:], g[:QK_ROPE // 2]])
            inv_freq = 1.0 / (ROPE_THETA ** (jnp.arange(0, QK_ROPE, 2, dtype=F32) / QK_ROPE))
            freq = jnp.tile(inv_freq, LANES // (QK_ROPE // 2)).reshape(1, LANES)
            sgn = jnp.tile(jnp.concatenate([-jnp.ones(QK_ROPE // 2, F32), jnp.ones(QK_ROPE // 2, F32)]),
                           LANES // QK_ROPE).reshape(1, LANES)
            ws = [_row(norm_w[i, 1]), wcq, wckv, wkr.astype(BF16), _swap_halves(wkr).astype(BF16),
                  _row(mla_q_a_norm[j]), _row(mla_kv_a_norm[j]), wqn, wqr.astype(BF16),
                  _swap_halves(wqr).astype(BF16), wkn, wvt,
                  tile_n(qg[:QK_NOPE]), tile_n(kg[:QK_NOPE]), tile_n(qg[QK_NOPE:]),
                  tile_n(swap(qg[QK_NOPE:])), tile_n(kg[QK_NOPE:]), tile_n(swap(kg[QK_NOPE:])),
                  freq, sgn]
            qn, qr, kn, kr, vt = _mla_in(xf, positions.reshape(t, 1).astype(jnp.int32), bsz, seq, ws)
            o = _attn(qn, qr, kn, kr, vt, bsz, seq)
            xf = ffn(xf, i, 1, attn=o, wo=mla_w_out[j].astype(BF16))
    return xf.reshape(bsz, seq, D_MODEL)
```

```python
import functools
import math

import jax
import jax.numpy as jnp
from jax import lax
from jax.experimental import pallas as pl
from jax.experimental.pallas import tpu as pltpu

F32 = jnp.float32
BF16 = jnp.bfloat16

D_MODEL = 1024
D_FF = 2816
EPS = 1e-6

SSM_HEADS = 32
SSM_HEAD_DIM = 64
SSM_GROUPS = 8
SSM_HPG = SSM_HEADS // SSM_GROUPS
SSM_STATE = 128
D_INNER = SSM_HEADS * SSM_HEAD_DIM
GROUP_COLS = SSM_HPG * SSM_HEAD_DIM
CONV_K = 4
CONV_DIM = D_INNER + 2 * SSM_GROUPS * SSM_STATE
CHUNK = 128

MLA_HEADS = 16
Q_LORA = 384
KV_LORA = 256
QK_NOPE = 64
QK_ROPE = 32
V_DIM = 64
QK_DIM = QK_NOPE + QK_ROPE
ROPE_THETA = 10000.0

LANES = 128
SUBLANES = 8
PERM_ROWS = CHUNK // SUBLANES
VMEM_LIMIT = 56 * 1024 * 1024

FFN_ROWS = 512
FFN_COLS = 256
SSM_IN_ROWS = 512
SSM_IN_COLS = 256
SSD_ROWS = 4 * CHUNK
MLA_IN_ROWS = 512
ATTN_BLOCK = 512
ATTN_QSUB = 256
ATTN_GROUP = 4


def _dot(a, b):
    return jnp.dot(a, b, preferred_element_type=F32)


def _dot_nt(a, b):
    return lax.dot_general(a, b, (((1,), (1,)), ((), ())), preferred_element_type=F32)


def _dot_tn(a, b):
    return lax.dot_general(a, b, (((0,), (0,)), ((), ())), preferred_element_type=F32)


def _rms(x, w):
    return x * lax.rsqrt(jnp.mean(x * x, axis=-1, keepdims=True) + EPS) * w


def _silu(x):
    return x * jax.nn.sigmoid(x)


def _split2(x):
    hi = x.astype(BF16)
    lo = (x - hi.astype(F32)).astype(BF16)
    return hi, lo


def _split3(x):
    hi = x.astype(BF16)
    r = x - hi.astype(F32)
    mid = r.astype(BF16)
    lo = (r - mid.astype(F32)).astype(BF16)
    return hi, mid, lo


def _const_spec(shape):
    zeros = (0,) * len(shape)
    return pl.BlockSpec(shape, lambda *_: zeros, pipeline_mode=pl.Buffered(1))


def _params(sem):
    return pltpu.CompilerParams(dimension_semantics=sem, vmem_limit_bytes=VMEM_LIMIT)


def _ffn_body(x, nw_ref, wg_ref, wu_ref, wd_ref, o_ref, acc_ref):
    h = _rms(x, nw_ref[...]).astype(BF16)
    for c in range(D_FF // FFN_COLS):
        sl = slice(c * FFN_COLS, (c + 1) * FFN_COLS)
        g = _dot(h, wg_ref[:, sl])
        u = _dot(h, wu_ref[:, sl])
        a = (_silu(g) * u).astype(BF16)
        contrib = _dot(a, wd_ref[sl, :])
        if c == 0:
            acc_ref[...] = contrib
        else:
            acc_ref[...] += contrib
    o_ref[...] = x + 0.5 * acc_ref[...]


def _ffn_kernel(x_ref, nw_ref, wg_ref, wu_ref, wd_ref, o_ref, acc_ref):
    _ffn_body(x_ref[...], nw_ref, wg_ref, wu_ref, wd_ref, o_ref, acc_ref)


def _ffn_proj_kernel(x_ref, a_ref, wo_ref, nw_ref, wg_ref, wu_ref, wd_ref, o_ref, acc_ref):
    x = x_ref[...] + _dot(a_ref[...], wo_ref[...])
    _ffn_body(x, nw_ref, wg_ref, wu_ref, wd_ref, o_ref, acc_ref)


def _ffn(x, nw, wg, wu, wd, attn=None, wo=None):
    t = x.shape[0]
    rows = min(FFN_ROWS, t)
    row_spec = pl.BlockSpec((rows, D_MODEL), lambda i: (i, 0))
    w_specs = [_const_spec((1, D_MODEL)), _const_spec((D_MODEL, D_FF)),
               _const_spec((D_MODEL, D_FF)), _const_spec((D_FF, D_MODEL))]
    if attn is None:
        kern, ins, specs = _ffn_kernel, (x,), [row_spec]
    else:
        kern, ins = _ffn_proj_kernel, (x, attn, wo)
        specs = [row_spec, row_spec, _const_spec((D_MODEL, D_MODEL))]
    return pl.pallas_call(
        kern,
        grid=(t // rows,),
        in_specs=specs + w_specs,
        out_specs=row_spec,
        out_shape=jax.ShapeDtypeStruct((t, D_MODEL), F32),
        scratch_shapes=[pltpu.VMEM((rows, D_MODEL), F32)],
        compiler_params=_params(("parallel",)),
        name="ffn" if attn is None else "ffn_proj",
    )(*ins, nw, wg, wu, wd)


def _ssm_in_kernel(x_ref, nw_ref, perm_ref, wz_ref, wxbc_ref, wdt_ref, cw_ref, cb_ref, dtb_ref,
                   z_ref, xbc_ref, dt_ref, carry):
    rows = x_ref.shape[0]
    nchunk = rows // CHUNK
    ntail = CONV_K - 1
    h_tok = _rms(x_ref[...], nw_ref[...]).astype(BF16)
    h = jnp.concatenate([_dot(perm_ref[...], h_tok[ch * CHUNK:(ch + 1) * CHUNK, :]).astype(BF16)
                         for ch in range(nchunk)], axis=0)

    @pl.when(pl.program_id(1) == 0)
    def _():
        carry[...] = jnp.zeros(carry.shape, F32)

    sub = lax.broadcasted_iota(jnp.int32, (SUBLANES, SSM_IN_COLS), 0)
    n_conv = CONV_DIM // SSM_IN_COLS
    n_z = D_INNER // SSM_IN_COLS
    for c in range(n_conv):
        sl = slice(c * SSM_IN_COLS, (c + 1) * SSM_IN_COLS)
        u = _dot(h, wxbc_ref[:, sl])
        if c % (n_conv // n_z) == 0:
            zsl = slice(c // (n_conv // n_z) * SSM_IN_COLS, (c // (n_conv // n_z) + 1) * SSM_IN_COLS)
            z_ref[:, zsl] = _dot(h, wz_ref[:, zsl]).astype(BF16)
        vrow = [[u[(ch * PERM_ROWS + i) * SUBLANES:(ch * PERM_ROWS + i + 1) * SUBLANES, :]
                 for i in range(PERM_ROWS)] for ch in range(nchunk)]
        wrapped = []
        for ch in range(nchunk):
            wr = {}
            for k in range(PERM_ROWS - ntail, PERM_ROWS):
                kk = k - (PERM_ROWS - ntail)
                prev = carry[kk * SUBLANES:(kk + 1) * SUBLANES, sl] if ch == 0 else vrow[ch - 1][k]
                wr[k] = jnp.where(sub == 0, pltpu.roll(prev, 1, axis=0), pltpu.roll(vrow[ch][k], 1, axis=0))
            wrapped.append(wr)
        acc = cb_ref[:, sl] + cw_ref[CONV_K - 1:CONV_K, sl] * u
        for back in range(1, CONV_K):
            shifted = jnp.concatenate(
                [vrow[ch][i - back] if i >= back else wrapped[ch][PERM_ROWS + i - back]
                 for ch in range(nchunk) for i in range(PERM_ROWS)], axis=0)
            acc = acc + cw_ref[CONV_K - 1 - back:CONV_K - back, sl] * shifted
        xbc_ref[:, sl] = _silu(acc).astype(BF16)
        for kk in range(ntail):
            carry[kk * SUBLANES:(kk + 1) * SUBLANES, sl] = vrow[nchunk - 1][PERM_ROWS - ntail + kk]
    dtv = _dot(h, wdt_ref[...]) + dtb_ref[...]
    dt_ref[...] = jnp.maximum(dtv, 0.0) + jnp.log1p(jnp.exp(-jnp.abs(dtv)))


def _ssm_in(x, bsz, seq, nw, perm, wz, wxbc, wdt, cw, cb, dtb):
    t = x.shape[0]
    rows = min(SSM_IN_ROWS, seq)
    ns = seq // rows
    row = lambda w: pl.BlockSpec((rows, w), lambda b, s: (b * ns + s, 0))
    return pl.pallas_call(
        _ssm_in_kernel,
        grid=(bsz, ns),
        in_specs=[row(D_MODEL), _const_spec((1, D_MODEL)), _const_spec((CHUNK, CHUNK)),
                  _const_spec((D_MODEL, D_INNER)),
                  _const_spec((D_MODEL, CONV_DIM)), _const_spec((D_MODEL, LANES)),
                  _const_spec((CONV_K, CONV_DIM)), _const_spec((1, CONV_DIM)),
                  _const_spec((1, LANES))],
        out_specs=[row(D_INNER), row(CONV_DIM), row(LANES)],
        out_shape=[jax.ShapeDtypeStruct((t, D_INNER), BF16),
                   jax.ShapeDtypeStruct((t, CONV_DIM), BF16),
                   jax.ShapeDtypeStruct((t, LANES), F32)],
        scratch_shapes=[pltpu.VMEM(((CONV_K - 1) * SUBLANES, CONV_DIM), F32)],
        compiler_params=_params(("parallel", "arbitrary")),
        name="ssm_in",
    )(x, nw, perm, wz, wxbc, wdt, cw, cb, dtb)


def _ssd_kernel(x_ref, z_ref, xbc_ref, dt_ref, a_ref, d_ref, nw_ref, tril_ref, unperm_ref, exp_ref,
                wo_ref, o_ref, state, ybuf):
    L = CHUNK

    @pl.when(pl.program_id(1) == 0)
    def _():
        state[...] = jnp.zeros(state.shape, F32)

    row = lax.broadcasted_iota(jnp.int32, (L, L), 0)
    col = lax.broadcasted_iota(jnp.int32, (L, L), 1)
    tok = lambda r: (r % SUBLANES) * PERM_ROWS + r // SUBLANES
    causal = tok(col) <= tok(row)
    head_of_lane = lax.broadcasted_iota(jnp.int32, (L, GROUP_COLS), 1) // SSM_HEAD_DIM

    for ck in range(x_ref.shape[0] // L):
        rsl = slice(ck * L, (ck + 1) * L)
        dt = dt_ref[rsl, :]
        dta = dt * a_ref[...]
        hi, mid, lo = _split3(dta)
        cs = _dot(tril_ref[...], jnp.concatenate([hi, mid, lo], axis=1))
        acs = cs[:, :LANES] + cs[:, LANES:2 * LANES] + cs[:, 2 * LANES:]
        acs_last = acs[L - 1:L, :]
        eacs = jnp.exp(acs)
        wgt = dt * jnp.exp(acs_last - acs)
        acs_src = (acs - jnp.log(dt)).T

        wgt_s = jnp.concatenate(_split2(wgt), axis=1)
        eacs_s = jnp.concatenate(_split2(eacs), axis=1)

        def group_front(g):
            gsl = slice(g * GROUP_COLS, (g + 1) * GROUP_COLS)
            wgt_x = _dot(wgt_s, exp_ref[:, gsl])
            eacs_x = _dot(eacs_s, exp_ref[:, gsl])
            xs_b = xbc_ref[rsl, gsl]
            xs = xs_b.astype(F32)
            bm = xbc_ref[rsl, D_INNER + g * SSM_STATE:D_INNER + (g + 1) * SSM_STATE]
            cm = xbc_ref[rsl, D_INNER + SSM_GROUPS * SSM_STATE + g * SSM_STATE:
                         D_INNER + SSM_GROUPS * SSM_STATE + (g + 1) * SSM_STATE]
            cb = _dot_nt(cm, bm)
            st = state[g]
            y_off = _dot(cm, st.astype(BF16)) * eacs_x
            state[g] = st * eacs_x[L - 1:L, :] + _dot_tn(bm, (xs * wgt_x).astype(BF16))
            return cb, y_off, xs_b, xs

        def group_back(g, cb, y_off, xs_b, xs):
            gsl = slice(g * GROUP_COLS, (g + 1) * GROUP_COLS)
            ms = []
            for r in range(SSM_HPG):
                hd = g * SSM_HPG + r
                seg = acs[:, hd:hd + 1] - acs_src[hd:hd + 1, :]
                decay_dt = jnp.exp(jnp.where(causal, seg, -jnp.inf))
                ms.append((cb * decay_dt).astype(BF16))
            x_diag = jnp.concatenate(
                [jnp.where(head_of_lane == r, xs_b, jnp.zeros_like(xs_b)) for r in range(SSM_HPG)], axis=0)
            y = _dot(jnp.concatenate(ms, axis=1), x_diag) + y_off + d_ref[:, gsl] * xs
            gt = y * _silu(z_ref[rsl, gsl].astype(F32))
            gt = gt * lax.rsqrt(jnp.mean(gt * gt, axis=-1, keepdims=True) + EPS)
            ybuf[rsl, gsl] = (gt * nw_ref[:, gsl]).astype(BF16)

        ahead = 1
        fronts = {g: group_front(g) for g in range(min(ahead, SSM_GROUPS))}
        for g in range(SSM_GROUPS):
            if g + ahead < SSM_GROUPS:
                fronts[g + ahead] = group_front(g + ahead)
            group_back(g, *fronts.pop(g))
        ybuf[rsl, :] = _dot(unperm_ref[...], ybuf[rsl, :]).astype(BF16)
    o_ref[...] = x_ref[...] + _dot(ybuf[...], wo_ref[...])


def _ssd(x, bsz, seq, z, xbc, dt, a_row, d_row, nw, tril, unperm, expand, wo):
    t = x.shape[0]
    rows = min(SSD_ROWS, seq)
    nc = seq // rows
    row = lambda w: pl.BlockSpec((rows, w), lambda b, c: (b * nc + c, 0))
    return pl.pallas_call(
        _ssd_kernel,
        grid=(bsz, nc),
        in_specs=[row(D_MODEL), row(D_INNER), row(CONV_DIM), row(LANES),
                  _const_spec((1, LANES)), _const_spec((1, D_INNER)), _const_spec((1, D_INNER)),
                  _const_spec((CHUNK, CHUNK)), _const_spec((CHUNK, CHUNK)), _const_spec((2 * LANES, D_INNER)),
                  _const_spec((D_INNER, D_MODEL))],
        out_specs=row(D_MODEL),
        out_shape=jax.ShapeDtypeStruct((t, D_MODEL), F32),
        scratch_shapes=[pltpu.VMEM((SSM_GROUPS, SSM_STATE, GROUP_COLS), F32),
                        pltpu.VMEM((rows, D_INNER), BF16)],
        compiler_params=_params(("parallel", "arbitrary")),
        name="ssd",
    )(x, z, xbc, dt, a_row, d_row, nw, tril, unperm, expand, wo)


def _group_rstd(sq, group):
    lane = lax.broadcasted_iota(jnp.int32, sq.shape, 1)
    res = jnp.zeros_like(sq)
    for gi in range(LANES // group):
        msk = (lane >= gi * group) & (lane < (gi + 1) * group)
        s = jnp.sum(jnp.where(msk, sq, 0.0), axis=-1, keepdims=True)
        res = jnp.where(msk, lax.rsqrt(s * (1.0 / group) + EPS), res)
    return res


def _mla_in_kernel(x_ref, pos_ref, nw_ref, wcq_ref, wckv_ref, wkr_ref, wkrs_ref,
                   qan_ref, kvan_ref, wqn_ref, wqr_ref, wqrs_ref, wkn_ref, wvt_ref,
                   gqn_ref, gkn_ref, gqr_ref, gqrs_ref, gkr_ref, gkrs_ref, freq_ref, sgn_ref,
                   rep_ref, sel_ref, qn_ref, qr_ref, kn_ref, kr_ref, vt_ref):
    h = _rms(x_ref[...], nw_ref[...]).astype(BF16)
    cq =_rms(_dot(h, wcq_ref[...]), qan_ref[...]).astype(BF16)
    ckv = _rms(_dot(h, wckv_ref[...]), kvan_ref[...]).astype(BF16)

    rows = x_ref.shape[0]
    ang = pos_ref[...].astype(F32) * freq_ref[...]
    tab = jnp.concatenate([jnp.cos(ang), jnp.sin(ang)], axis=1)
    rep = _dot(rep_ref[...], jnp.concatenate(_split2(tab), axis=0))
    tok8 = lax.broadcasted_iota(jnp.int32, (rows, 2 * LANES), 0) % SUBLANES
    grp = (lax.broadcasted_iota(jnp.int32, (rows, 2 * LANES), 1) % LANES) // (QK_ROPE // 2)
    own = jnp.where(tok8 == grp, rep, 0.0)
    cos = _dot(jnp.concatenate(_split2(own[:, :LANES]), axis=1), sel_ref[...])
    sin = _dot(jnp.concatenate(_split2(own[:, LANES:]), axis=1), sel_ref[...]) * sgn_ref[...]

    qscale = QK_DIM ** -0.5 * math.log2(math.e)

    vt_ref[...] = _dot_nt(wvt_ref[...], ckv).astype(BF16)
    for c in range(MLA_HEADS * QK_NOPE // LANES):
        sl = slice(c * LANES, (c + 1) * LANES)
        qn = _dot(cq, wqn_ref[:, sl])
        qn_ref[:, sl] = (qn * _group_rstd(qn * qn, QK_NOPE) * (gqn_ref[...] * qscale)).astype(BF16)
        kn = _dot(ckv, wkn_ref[:, sl])
        kn_ref[:, sl] = (kn * _group_rstd(kn * kn, QK_NOPE) * gkn_ref[...]).astype(BF16)
    for c in range(MLA_HEADS * QK_ROPE // LANES):
        sl = slice(c * LANES, (c + 1) * LANES)
        qr = _dot(cq, wqr_ref[:, sl])
        qrs = _dot(cq, wqrs_ref[:, sl])
        rot = qr * gqr_ref[...] * cos + qrs * gqrs_ref[...] * sin
        qr_ref[:, sl] = (rot * _group_rstd(qr * qr, QK_ROPE) * qscale).astype(BF16)
    kr = _dot(h, wkr_ref[...])
    krs = _dot(h, wkrs_ref[...])
    rot = kr * gkr_ref[...] * cos + krs * gkrs_ref[...] * sin
    kr_ref[...] = (rot * _group_rstd(kr * kr, QK_ROPE)).astype(BF16)


def _mla_in(x, pos, bsz, seq, ws):
    t = x.shape[0]
    rows = min(MLA_IN_ROWS, seq)
    ns = seq // rows
    row = lambda w: pl.BlockSpec((rows, w), lambda i: (i, 0))
    nope = MLA_HEADS * QK_NOPE
    rope = MLA_HEADS * QK_ROPE
    vdim = MLA_HEADS * V_DIM
    shapes = [(1, D_MODEL), (D_MODEL, Q_LORA), (D_MODEL, KV_LORA), (D_MODEL, LANES), (D_MODEL, LANES),
              (1, Q_LORA), (1, KV_LORA), (Q_LORA, nope), (Q_LORA, rope), (Q_LORA, rope),
              (KV_LORA, nope), (vdim, KV_LORA)] + [(1, LANES)] * 8 + [(rows, rows // 4), (2 * LANES, LANES)]
    tok = jnp.arange(rows, dtype=jnp.int32)
    rep = (tok[:, None] // SUBLANES == jnp.arange(rows // 4, dtype=jnp.int32)[None, :] % (rows // SUBLANES))
    lane2 = jnp.arange(2 * LANES, dtype=jnp.int32)
    sel = (lane2[:, None] % (QK_ROPE // 2) == jnp.arange(LANES, dtype=jnp.int32)[None, :] % (QK_ROPE // 2))
    pos_c = jnp.repeat(pos.reshape(t // SUBLANES, SUBLANES), LANES // SUBLANES, axis=1)
    vt_spec = pl.BlockSpec((None, vdim, rows), lambda i: (i // ns, 0, i % ns))
    return pl.pallas_call(
        _mla_in_kernel,
        grid=(t // rows,),
        in_specs=[row(D_MODEL), pl.BlockSpec((rows // SUBLANES, LANES), lambda i: (i, 0))]
        + [_const_spec(s) for s in shapes],
        out_specs=[row(nope), row(rope), row(nope), row(LANES), vt_spec],
        out_shape=[jax.ShapeDtypeStruct((t, nope), BF16), jax.ShapeDtypeStruct((t, rope), BF16),
                   jax.ShapeDtypeStruct((t, nope), BF16), jax.ShapeDtypeStruct((t, LANES), BF16),
                   jax.ShapeDtypeStruct((bsz, vdim, seq), BF16)],
        compiler_params=_params(("parallel",)),
        name="mla_in",
    )(x, pos_c, *ws, rep.astype(BF16), sel.astype(BF16))


def _attn_kernel(qn_ref, qr_ref, kn_ref, kr_ref, vt_ref, o_ref, m_ref, l_ref, acc_ref):
    blk = qn_ref.shape[0]
    p = pl.program_id(1)
    i = pl.program_id(2)
    lane = lax.broadcasted_iota(jnp.int32, (blk, LANES), 1)
    qn = qn_ref[...]
    qr = qr_ref[...]
    zero = jnp.zeros_like(qn)
    rope_lo = (p % 2) * (2 * QK_ROPE)
    qs = []
    for hh in range(2):
        nope_part = jnp.where((lane >= hh * QK_NOPE) & (lane < (hh + 1) * QK_NOPE), qn, zero)
        lo = rope_lo + hh * QK_ROPE
        rope_part = jnp.where((lane >= lo) & (lane < lo + QK_ROPE), qr, zero)
        qs.append(jnp.concatenate([nope_part, rope_part], axis=1))

    m_ref[...] = jnp.full(m_ref.shape, -jnp.inf, F32)
    l_ref[...] = jnp.zeros(l_ref.shape, F32)
    acc_ref[...] = jnp.zeros(acc_ref.shape, F32)

    chains = [(hh, q0) for hh in range(2) for q0 in range(0, blk, ATTN_QSUB)]

    def process(blocks):
        starts = [pl.multiple_of(j * blk, blk) for j, _ in blocks]
        nkeys = lambda masked, q0: q0 + ATTN_QSUB if masked else blk
        scores = []
        for (_, masked), start in zip(blocks, starts):
            kfull = jnp.concatenate([kn_ref[pl.ds(start, blk), :], kr_ref[pl.ds(start, blk), :]], axis=1)
            scores.append([_dot_nt(kfull[:nkeys(masked, q0), :], qs[hh][q0:q0 + ATTN_QSUB, :])
                           for hh, q0 in chains])
        for (_, masked), start, block_scores in zip(blocks, starts, scores):
            vt_blk = vt_ref[:, pl.ds(start, blk)]
            for (hh, q0), s in zip(chains, block_scores):
                qsl = slice(q0, q0 + ATTN_QSUB)
                vt = vt_blk[:, :nkeys(masked, q0)]
                if masked:
                    key = lax.broadcasted_iota(jnp.int32, s.shape, 0)
                    qry = lax.broadcasted_iota(jnp.int32, s.shape, 1) + q0
                    s = jnp.where(key <= qry, s, -jnp.inf)
                m_old = m_ref[hh, :, qsl]
                m_new = jnp.maximum(m_old, jnp.max(s, axis=0, keepdims=True))
                alpha = jnp.exp2(m_old - m_new)
                pr = jnp.exp2(s - m_new)
                l_ref[hh, :, qsl] = alpha * l_ref[hh, :, qsl] + jnp.sum(pr, axis=0, keepdims=True)
                pv = _dot(vt, pr.astype(BF16))
                acc_ref[hh, :, qsl] = (alpha * acc_ref[hh, :, qsl]
                                       + pv[hh * V_DIM:(hh + 1) * V_DIM, :])
                m_ref[hh, :, qsl] = m_new

    grp = ATTN_GROUP
    n_full = i // grp

    def body(jj, carry):
        process([(grp * jj + k, False) for k in range(grp)])
        return carry

    lax.fori_loop(0, n_full, body, 0)

    for rem in range(grp):
        @pl.when(i - grp * n_full == rem)
        def _():
            process([(i - rem + k, False) for k in range(rem)] + [(i, True)])

    ot = jnp.concatenate([acc_ref[0] / l_ref[0], acc_ref[1] / l_ref[1]], axis=0)
    o_ref[...] = ot.T.astype(BF16)


def _attn(qn, qr, kn, kr, vt, bsz, seq):
    t = qn.shape[0]
    blk = min(ATTN_BLOCK, seq)
    nq = seq // blk
    npair = MLA_HEADS // 2
    qspec = pl.BlockSpec((blk, LANES), lambda b, p, i: (b * nq + i, p))
    qrspec = pl.BlockSpec((blk, LANES), lambda b, p, i: (b * nq + i, p // 2))
    kspec = pl.BlockSpec((seq, LANES), lambda b, p, i: (b, p))
    krspec = pl.BlockSpec((seq, LANES), lambda b, p, i: (b, 0))
    vtspec = pl.BlockSpec((None, 2 * V_DIM, seq), lambda b, p, i: (b, p, 0))
    return pl.pallas_call(
        _attn_kernel,
        grid=(bsz, npair, nq),
        in_specs=[qspec, qrspec, kspec, krspec, vtspec],
        out_specs=qspec,
        out_shape=jax.ShapeDtypeStruct((t, MLA_HEADS * V_DIM), BF16),
        scratch_shapes=[pltpu.VMEM((2, 1, blk), F32), pltpu.VMEM((2, 1, blk), F32),
                        pltpu.VMEM((2, V_DIM, blk), F32)],
        compiler_params=_params(("parallel", "parallel", "arbitrary")),
        name="attn",
    )(qn, qr, kn, kr, vt)


def _row(v, width=None):
    v = v.astype(F32).reshape(1, -1)
    if width is not None and v.shape[1] < width:
        v = jnp.pad(v, ((0, 0), (0, width - v.shape[1])))
    return v


def _swap_halves(w):
    k, n = w.shape
    w = w.reshape(k, n // QK_ROPE, 2, QK_ROPE // 2)
    return w[:, :, ::-1, :].reshape(k, n)


def kernel(x, positions, norm_w, ffn_w_gate, ffn_w_up, ffn_w_down, ssm_w_in, ssm_conv_w, ssm_conv_b, ssm_dt_bias, ssm_a_log, ssm_d, ssm_norm_w, ssm_w_out, mla_w_in, mla_q_a_norm, mla_kv_a_norm, mla_w_q_b, mla_w_kv_b, mla_q_norm, mla_k_norm, mla_w_out):
    bsz, seq, _ = x.shape
    t = bsz * seq
    depth = norm_w.shape[0]
    xf = x.reshape(t, D_MODEL)

    def ffn(xin, i, k, attn=None, wo=None):
        return _ffn(xin, _row(norm_w[i, 0 if k == 0 else 2]),
                    ffn_w_gate[i, k].astype(BF16), ffn_w_up[i, k].astype(BF16),
                    ffn_w_down[i, k].astype(BF16), attn, wo)

    for i in range(depth):
        xf = ffn(xf, i, 0)
        j = i // 2
        if i % 2 == 0:
            w_in = ssm_w_in[j]
            wz = w_in[:, :D_INNER].astype(BF16)
            wxbc = w_in[:, D_INNER:D_INNER + CONV_DIM].astype(BF16)
            wdt = jnp.pad(w_in[:, D_INNER + CONV_DIM:], ((0, 0), (0, LANES - SSM_HEADS))).astype(BF16)
            ridx = jnp.arange(CHUNK, dtype=jnp.int32)
            tok = (ridx % SUBLANES) * PERM_ROWS + ridx // SUBLANES
            perm = (tok[:, None] == ridx[None, :]).astype(BF16)
            z, xbc, dt = _ssm_in(xf, bsz, seq, _row(norm_w[i, 1]), perm, wz, wxbc, wdt,
                                 ssm_conv_w[j].astype(F32), _row(ssm_conv_b[j]),
                                 _row(ssm_dt_bias[j], LANES))
            a_row = _row(-jnp.exp(ssm_a_log[j].astype(F32)), LANES)
            d_row = _row(jnp.repeat(ssm_d[j].astype(F32), SSM_HEAD_DIM))
            tril = (tok[None, :] <= tok[:, None]).astype(BF16)
            head_of_col = jnp.arange(D_INNER, dtype=jnp.int32) // SSM_HEAD_DIM
            expand = (jnp.arange(2 * LANES, dtype=jnp.int32)[:, None] % LANES
                      == head_of_col[None, :]).astype(BF16)
            xf = _ssd(xf, bsz, seq, z, xbc, dt, a_row, d_row, _row(ssm_norm_w[j]), tril, perm.T, expand,
                      ssm_w_out[j].astype(BF16))
            xf = ffn(xf, i, 1)
        else:
            w_in = mla_w_in[j]
            wcq = w_in[:, :Q_LORA].astype(BF16)
            wckv = w_in[:, Q_LORA:Q_LORA + KV_LORA].astype(BF16)
            wkr = jnp.tile(w_in[:, Q_LORA + KV_LORA:], (1, LANES // QK_ROPE))
            wq = mla_w_q_b[j].reshape(Q_LORA, MLA_HEADS, QK_DIM)
            wqn = wq[:, :, :QK_NOPE].reshape(Q_LORA, MLA_HEADS * QK_NOPE).astype(BF16)
            wqr = wq[:, :, QK_NOPE:].reshape(Q_LORA, MLA_HEADS * QK_ROPE)
            wkv = mla_w_kv_b[j].reshape(KV_LORA, MLA_HEADS, QK_NOPE + V_DIM)
            wkn = wkv[:, :, :QK_NOPE].reshape(KV_LORA, MLA_HEADS * QK_NOPE).astype(BF16)
            wvt = wkv[:, :, QK_NOPE:].reshape(KV_LORA, MLA_HEADS * V_DIM).T.astype(BF16)
            qg, kg = mla_q_norm[j].astype(F32), mla_k_norm[j].astype(F32)
            tile_n = lambda g: jnp.tile(g, LANES // g.shape[0]).reshape(1, LANES)
            swap = lambda g: jnp.concatenate([g[QK_ROPE // 2:], g[:QK_ROPE // 2]])
            inv_freq = 1.0 / (ROPE_THETA ** (jnp.arange(0, QK_ROPE, 2, dtype=F32) / QK_ROPE))
            freq = jnp.tile(inv_freq, LANES // (QK_ROPE // 2)).reshape(1, LANES)
            sgn = jnp.tile(jnp.concatenate([-jnp.ones(QK_ROPE // 2, F32), jnp.ones(QK_ROPE // 2, F32)]),
                           LANES // QK_ROPE).reshape(1, LANES)
            ws = [_row(norm_w[i, 1]), wcq, wckv, wkr.astype(BF16), _swap_halves(wkr).astype(BF16),
                  _row(mla_q_a_norm[j]), _row(mla_kv_a_norm[j]), wqn, wqr.astype(BF16),
                  _swap_halves(wqr).astype(BF16), wkn, wvt,
                  tile_n(qg[:QK_NOPE]), tile_n(kg[:QK_NOPE]), tile_n(qg[QK_NOPE:]),
                  tile_n(swap(qg[QK_NOPE:])), tile_n(kg[QK_NOPE:]), tile_n(swap(kg[QK_NOPE:])),
                  freq, sgn]
            qn, qr, kn, kr, vt = _mla_in(xf, positions.reshape(t).astype(jnp.int32), bsz, seq, ws)
            o = _attn(qn, qr, kn, kr, vt, bsz, seq)
            xf = ffn(xf, i, 1, attn=o, wo=mla_w_out[j].astype(BF16))
    return xf.reshape(bsz, seq, D_MODEL)
```

```python
import functools
import math

import jax
import jax.numpy as jnp
from jax import lax
from jax.experimental import pallas as pl
from jax.experimental.pallas import tpu as pltpu

F32 = jnp.float32
BF16 = jnp.bfloat16

D_MODEL = 1024
D_FF = 2816
EPS = 1e-6

SSM_HEADS = 32
SSM_HEAD_DIM = 64
SSM_GROUPS = 8
SSM_HPG = SSM_HEADS // SSM_GROUPS
SSM_STATE = 128
D_INNER = SSM_HEADS * SSM_HEAD_DIM
GROUP_COLS = SSM_HPG * SSM_HEAD_DIM
CONV_K = 4
CONV_DIM = D_INNER + 2 * SSM_GROUPS * SSM_STATE
CHUNK = 128

MLA_HEADS = 16
Q_LORA = 384
KV_LORA = 256
QK_NOPE = 64
QK_ROPE = 32
V_DIM = 64
QK_DIM = QK_NOPE + QK_ROPE
ROPE_THETA = 10000.0

LANES = 128
SUBLANES = 8
PERM_ROWS = CHUNK // SUBLANES
VMEM_LIMIT = 56 * 1024 * 1024

FFN_ROWS = 1024
FFN_COLS = 256
SSM_IN_ROWS = 512
SSM_IN_COLS = 256
SSD_ROWS = 4 * CHUNK
MLA_IN_ROWS = 512
ATTN_BLOCK = 512
ATTN_QSUB = 256
ATTN_AHEAD = 4
ATTN_GROUP = 4


def _dot(a, b):
    return jnp.dot(a, b, preferred_element_type=F32)


def _dot_nt(a, b):
    return lax.dot_general(a, b, (((1,), (1,)), ((), ())), preferred_element_type=F32)


def _dot_tn(a, b):
    return lax.dot_general(a, b, (((0,), (0,)), ((), ())), preferred_element_type=F32)


def _rms(x, w):
    return x * lax.rsqrt(jnp.mean(x * x, axis=-1, keepdims=True) + EPS) * w


def _silu(x):
    hx = 0.5 * x
    return hx * jnp.tanh(hx) + hx


def _split2(x):
    hi = x.astype(BF16)
    lo = (x - hi.astype(F32)).astype(BF16)
    return hi, lo


def _split3(x):
    hi = x.astype(BF16)
    r = x - hi.astype(F32)
    mid = r.astype(BF16)
    lo = (r - mid.astype(F32)).astype(BF16)
    return hi, mid, lo


def _const_spec(shape):
    zeros = (0,) * len(shape)
    return pl.BlockSpec(shape, lambda *_: zeros, pipeline_mode=pl.Buffered(1))


def _params(sem):
    return pltpu.CompilerParams(dimension_semantics=sem, vmem_limit_bytes=VMEM_LIMIT)


def _ffn_body(x, nw_ref, wg_ref, wu_ref, wd_ref, o_ref, acc_ref):
    h = _rms(x, nw_ref[...]).astype(BF16)
    for c in range(D_FF // FFN_COLS):
        sl = slice(c * FFN_COLS, (c + 1) * FFN_COLS)
        g = _dot(h, wg_ref[:, sl])
        u = _dot(h, wu_ref[:, sl])
        a = (_silu(g) * u).astype(BF16)
        contrib = _dot(a, wd_ref[sl, :])
        if c == 0:
            acc_ref[...] = contrib
        else:
            acc_ref[...] += contrib
    o_ref[...] = x + 0.5 * acc_ref[...]


def _ffn_kernel(x_ref, nw_ref, wg_ref, wu_ref, wd_ref, o_ref, acc_ref):
    _ffn_body(x_ref[...], nw_ref, wg_ref, wu_ref, wd_ref, o_ref, acc_ref)


def _ffn_proj_kernel(x_ref, a_ref, wo_ref, nw_ref, wg_ref, wu_ref, wd_ref, o_ref, acc_ref):
    x = x_ref[...] + _dot(a_ref[...], wo_ref[...])
    _ffn_body(x, nw_ref, wg_ref, wu_ref, wd_ref, o_ref, acc_ref)


def _ffn(x, nw, wg, wu, wd, attn=None, wo=None):
    t = x.shape[0]
    rows = min(FFN_ROWS, t)
    row_spec = pl.BlockSpec((rows, D_MODEL), lambda i: (i, 0))
    w_specs = [_const_spec((1, D_MODEL)), _const_spec((D_MODEL, D_FF)),
               _const_spec((D_MODEL, D_FF)), _const_spec((D_FF, D_MODEL))]
    if attn is None:
        kern, ins, specs = _ffn_kernel, (x,), [row_spec]
    else:
        kern, ins = _ffn_proj_kernel, (x, attn, wo)
        specs = [row_spec, row_spec, _const_spec((D_MODEL, D_MODEL))]
    return pl.pallas_call(
        kern,
        grid=(t // rows,),
        in_specs=specs + w_specs,
        out_specs=row_spec,
        out_shape=jax.ShapeDtypeStruct((t, D_MODEL), F32),
        scratch_shapes=[pltpu.VMEM((rows, D_MODEL), F32)],
        compiler_params=_params(("parallel",)),
        name="ffn" if attn is None else "ffn_proj",
    )(*ins, nw, wg, wu, wd)


def _ssm_in_kernel(x_ref, nw_ref, perm_ref, wz_ref, wxbc_ref, wdt_ref, cw_ref, cb_ref, dtb_ref,
                   z_ref, xbc_ref, dt_ref, carry):
    rows = x_ref.shape[0]
    nchunk = rows // CHUNK
    ntail = CONV_K - 1
    h_tok = _rms(x_ref[...], nw_ref[...]).astype(BF16)
    h = jnp.concatenate([_dot(perm_ref[...], h_tok[ch * CHUNK:(ch + 1) * CHUNK, :]).astype(BF16)
                         for ch in range(nchunk)], axis=0)

    @pl.when(pl.program_id(1) == 0)
    def _():
        carry[...] = jnp.zeros(carry.shape, F32)

    sub = lax.broadcasted_iota(jnp.int32, (SUBLANES, SSM_IN_COLS), 0)
    n_conv = CONV_DIM // SSM_IN_COLS
    n_z = D_INNER // SSM_IN_COLS
    for c in range(n_conv):
        sl = slice(c * SSM_IN_COLS, (c + 1) * SSM_IN_COLS)
        u = _dot(h, wxbc_ref[:, sl])
        if c % (n_conv // n_z) == 0:
            zsl = slice(c // (n_conv // n_z) * SSM_IN_COLS, (c // (n_conv // n_z) + 1) * SSM_IN_COLS)
            z_ref[:, zsl] = _dot(h, wz_ref[:, zsl]).astype(BF16)
        vrow = [[u[(ch * PERM_ROWS + i) * SUBLANES:(ch * PERM_ROWS + i + 1) * SUBLANES, :]
                 for i in range(PERM_ROWS)] for ch in range(nchunk)]
        wrapped = []
        for ch in range(nchunk):
            wr = {}
            for k in range(PERM_ROWS - ntail, PERM_ROWS):
                kk = k - (PERM_ROWS - ntail)
                prev = carry[kk * SUBLANES:(kk + 1) * SUBLANES, sl] if ch == 0 else vrow[ch - 1][k]
                wr[k] = jnp.where(sub == 0, pltpu.roll(prev, 1, axis=0), pltpu.roll(vrow[ch][k], 1, axis=0))
            wrapped.append(wr)
        acc = cb_ref[:, sl] + cw_ref[CONV_K - 1:CONV_K, sl] * u
        for back in range(1, CONV_K):
            shifted = jnp.concatenate(
                [vrow[ch][i - back] if i >= back else wrapped[ch][PERM_ROWS + i - back]
                 for ch in range(nchunk) for i in range(PERM_ROWS)], axis=0)
            acc = acc + cw_ref[CONV_K - 1 - back:CONV_K - back, sl] * shifted
        xbc_ref[:, sl] = _silu(acc).astype(BF16)
        for kk in range(ntail):
            carry[kk * SUBLANES:(kk + 1) * SUBLANES, sl] = vrow[nchunk - 1][PERM_ROWS - ntail + kk]
    dtv = _dot(h, wdt_ref[...]) + dtb_ref[...]
    dt_ref[...] = jnp.maximum(dtv, 0.0) + jnp.log1p(jnp.exp(-jnp.abs(dtv)))


def _ssm_in(x, bsz, seq, nw, perm, wz, wxbc, wdt, cw, cb, dtb):
    t = x.shape[0]
    rows = min(SSM_IN_ROWS, seq)
    ns = seq // rows
    row = lambda w: pl.BlockSpec((rows, w), lambda b, s: (b * ns + s, 0))
    return pl.pallas_call(
        _ssm_in_kernel,
        grid=(bsz, ns),
        in_specs=[row(D_MODEL), _const_spec((1, D_MODEL)), _const_spec((CHUNK, CHUNK)),
                  _const_spec((D_MODEL, D_INNER)),
                  _const_spec((D_MODEL, CONV_DIM)), _const_spec((D_MODEL, LANES)),
                  _const_spec((CONV_K, CONV_DIM)), _const_spec((1, CONV_DIM)),
                  _const_spec((1, LANES))],
        out_specs=[row(D_INNER), row(CONV_DIM), row(LANES)],
        out_shape=[jax.ShapeDtypeStruct((t, D_INNER), BF16),
                   jax.ShapeDtypeStruct((t, CONV_DIM), BF16),
                   jax.ShapeDtypeStruct((t, LANES), F32)],
        scratch_shapes=[pltpu.VMEM(((CONV_K - 1) * SUBLANES, CONV_DIM), F32)],
        compiler_params=_params(("parallel", "arbitrary")),
        name="ssm_in",
    )(x, nw, perm, wz, wxbc, wdt, cw, cb, dtb)


def _ssd_kernel(x_ref, z_ref, xbc_ref, dt_ref, a_ref, d_ref, nw_ref, tril_ref, unperm_ref, exp_ref,
                wo_ref, o_ref, state, ybuf):
    L = CHUNK

    @pl.when(pl.program_id(1) == 0)
    def _():
        state[...] = jnp.zeros(state.shape, F32)

    row = lax.broadcasted_iota(jnp.int32, (L, L), 0)
    col = lax.broadcasted_iota(jnp.int32, (L, L), 1)
    tok = lambda r: (r % SUBLANES) * PERM_ROWS + r // SUBLANES
    causal = tok(col) <= tok(row)
    head_of_lane = lax.broadcasted_iota(jnp.int32, (L, GROUP_COLS), 1) // SSM_HEAD_DIM

    for ck in range(x_ref.shape[0] // L):
        rsl = slice(ck * L, (ck + 1) * L)
        dt = dt_ref[rsl, :]
        dta = dt * a_ref[...]
        hi, mid, lo = _split3(dta)
        cs = _dot(tril_ref[...], jnp.concatenate([hi, mid, lo], axis=1))
        acs = cs[:, :LANES] + cs[:, LANES:2 * LANES] + cs[:, 2 * LANES:]
        acs_last = acs[L - 1:L, :]
        eacs = jnp.exp(acs)
        wgt = dt * jnp.exp(acs_last - acs)
        acs_src = (acs - jnp.log(dt)).T

        wgt_s = jnp.concatenate(_split2(wgt), axis=1)
        eacs_s = jnp.concatenate(_split2(eacs), axis=1)

        def group_front(g):
            gsl = slice(g * GROUP_COLS, (g + 1) * GROUP_COLS)
            wgt_x = _dot(wgt_s, exp_ref[:, gsl])
            eacs_x = _dot(eacs_s, exp_ref[:, gsl])
            xs_b = xbc_ref[rsl, gsl]
            xs = xs_b.astype(F32)
            bm = xbc_ref[rsl, D_INNER + g * SSM_STATE:D_INNER + (g + 1) * SSM_STATE]
            cm = xbc_ref[rsl, D_INNER + SSM_GROUPS * SSM_STATE + g * SSM_STATE:
                         D_INNER + SSM_GROUPS * SSM_STATE + (g + 1) * SSM_STATE]
            cb = _dot_nt(cm, bm)
            st = state[g]
            y_off = _dot(cm, st.astype(BF16)) * eacs_x
            state[g] = st * eacs_x[L - 1:L, :] + _dot_tn(bm, (xs * wgt_x).astype(BF16))
            return cb, y_off, xs_b, xs

        def group_back(g, cb, y_off, xs_b, xs):
            gsl = slice(g * GROUP_COLS, (g + 1) * GROUP_COLS)
            ms = []
            for r in range(SSM_HPG):
                hd = g * SSM_HPG + r
                seg = acs[:, hd:hd + 1] - acs_src[hd:hd + 1, :]
                decay_dt = jnp.exp(jnp.where(causal, seg, -jnp.inf))
                ms.append((cb * decay_dt).astype(BF16))
            x_diag = jnp.concatenate(
                [jnp.where(head_of_lane == r, xs_b, jnp.zeros_like(xs_b)) for r in range(SSM_HPG)], axis=0)
            y = _dot(jnp.concatenate(ms, axis=1), x_diag) + y_off + d_ref[:, gsl] * xs
            gt = y * _silu(z_ref[rsl, gsl].astype(F32))
            gt = gt * lax.rsqrt(jnp.mean(gt * gt, axis=-1, keepdims=True) + EPS)
            ybuf[rsl, gsl] = (gt * nw_ref[:, gsl]).astype(BF16)

        ahead = 1
        fronts = {g: group_front(g) for g in range(min(ahead, SSM_GROUPS))}
        for g in range(SSM_GROUPS):
            if g + ahead < SSM_GROUPS:
                fronts[g + ahead] = group_front(g + ahead)
            group_back(g, *fronts.pop(g))
        ybuf[rsl, :] = _dot(unperm_ref[...], ybuf[rsl, :]).astype(BF16)
    o_ref[...] = x_ref[...] + _dot(ybuf[...], wo_ref[...])


def _ssd(x, bsz, seq, z, xbc, dt, a_row, d_row, nw, tril, unperm, expand, wo):
    t = x.shape[0]
    rows = min(SSD_ROWS, seq)
    nc = seq // rows
    row = lambda w: pl.BlockSpec((rows, w), lambda b, c: (b * nc + c, 0))
    return pl.pallas_call(
        _ssd_kernel,
        grid=(bsz, nc),
        in_specs=[row(D_MODEL), row(D_INNER), row(CONV_DIM), row(LANES),
                  _const_spec((1, LANES)), _const_spec((1, D_INNER)), _const_spec((1, D_INNER)),
                  _const_spec((CHUNK, CHUNK)), _const_spec((CHUNK, CHUNK)), _const_spec((2 * LANES, D_INNER)),
                  _const_spec((D_INNER, D_MODEL))],
        out_specs=row(D_MODEL),
        out_shape=jax.ShapeDtypeStruct((t, D_MODEL), F32),
        scratch_shapes=[pltpu.VMEM((SSM_GROUPS, SSM_STATE, GROUP_COLS), F32),
                        pltpu.VMEM((rows, D_INNER), BF16)],
        compiler_params=_params(("parallel", "arbitrary")),
        name="ssd",
    )(x, z, xbc, dt, a_row, d_row, nw, tril, unperm, expand, wo)


def _group_rstd(sq, group):
    lane = lax.broadcasted_iota(jnp.int32, sq.shape, 1)
    res = jnp.zeros_like(sq)
    for gi in range(LANES // group):
        msk = (lane >= gi * group) & (lane < (gi + 1) * group)
        s = jnp.sum(jnp.where(msk, sq, 0.0), axis=-1, keepdims=True)
        res = jnp.where(msk, lax.rsqrt(s * (1.0 / group) + EPS), res)
    return res


def _mla_in_kernel(x_ref, pos_ref, nw_ref, wcq_ref, wckv_ref, wkr_ref, wkrs_ref,
                   qan_ref, kvan_ref, wqn_ref, wqr_ref, wqrs_ref, wkn_ref, wvt_ref,
                   gqn_ref, gkn_ref, gqr_ref, gqrs_ref, gkr_ref, gkrs_ref, freq_ref, sgn_ref,
                   rep_ref, sel_ref, qn_ref, qr_ref, kn_ref, kr_ref, vt_ref):
    h = _rms(x_ref[...], nw_ref[...]).astype(BF16)
    cq =_rms(_dot(h, wcq_ref[...]), qan_ref[...]).astype(BF16)
    ckv = _rms(_dot(h, wckv_ref[...]), kvan_ref[...]).astype(BF16)

    rows = x_ref.shape[0]
    ang = pos_ref[...].astype(F32) * freq_ref[...]
    tab = jnp.concatenate([jnp.cos(ang), jnp.sin(ang)], axis=1)
    rep = _dot(rep_ref[...], jnp.concatenate(_split2(tab), axis=0))
    tok8 = lax.broadcasted_iota(jnp.int32, (rows, 2 * LANES), 0) % SUBLANES
    grp = (lax.broadcasted_iota(jnp.int32, (rows, 2 * LANES), 1) % LANES) // (QK_ROPE // 2)
    own = jnp.where(tok8 == grp, rep, 0.0)
    cos = _dot(jnp.concatenate(_split2(own[:, :LANES]), axis=1), sel_ref[...])
    sin = _dot(jnp.concatenate(_split2(own[:, LANES:]), axis=1), sel_ref[...]) * sgn_ref[...]

    qscale = QK_DIM ** -0.5 * math.log2(math.e)

    vt_ref[...] = _dot_nt(wvt_ref[...], ckv).astype(BF16)
    for c in range(MLA_HEADS * QK_NOPE // LANES):
        sl = slice(c * LANES, (c + 1) * LANES)
        qn = _dot(cq, wqn_ref[:, sl])
        qn_ref[:, sl] = (qn * _group_rstd(qn * qn, QK_NOPE) * (gqn_ref[...] * qscale)).astype(BF16)
        kn = _dot(ckv, wkn_ref[:, sl])
        kn_ref[:, sl] = (kn * _group_rstd(kn * kn, QK_NOPE) * gkn_ref[...]).astype(BF16)
    for c in range(MLA_HEADS * QK_ROPE // LANES):
        sl = slice(c * LANES, (c + 1) * LANES)
        qr = _dot(cq, wqr_ref[:, sl])
        qrs = _dot(cq, wqrs_ref[:, sl])
        rot = qr * gqr_ref[...] * cos + qrs * gqrs_ref[...] * sin
        qr_ref[:, sl] = (rot * _group_rstd(qr * qr, QK_ROPE) * qscale).astype(BF16)
    kr = _dot(h, wkr_ref[...])
    krs = _dot(h, wkrs_ref[...])
    rot = kr * gkr_ref[...] * cos + krs * gkrs_ref[...] * sin
    kr_ref[...] = (rot * _group_rstd(kr * kr, QK_ROPE)).astype(BF16)


def _mla_in(x, pos, bsz, seq, ws):
    t = x.shape[0]
    rows = min(MLA_IN_ROWS, seq)
    ns = seq // rows
    row = lambda w: pl.BlockSpec((rows, w), lambda i: (i, 0))
    nope = MLA_HEADS * QK_NOPE
    rope = MLA_HEADS * QK_ROPE
    vdim = MLA_HEADS * V_DIM
    shapes = [(1, D_MODEL), (D_MODEL, Q_LORA), (D_MODEL, KV_LORA), (D_MODEL, LANES), (D_MODEL, LANES),
              (1, Q_LORA), (1, KV_LORA), (Q_LORA, nope), (Q_LORA, rope), (Q_LORA, rope),
              (KV_LORA, nope), (vdim, KV_LORA)] + [(1, LANES)] * 8 + [(rows, rows // 4), (2 * LANES, LANES)]
    tok = jnp.arange(rows, dtype=jnp.int32)
    rep = (tok[:, None] // SUBLANES == jnp.arange(rows // 4, dtype=jnp.int32)[None, :] % (rows // SUBLANES))
    lane2 = jnp.arange(2 * LANES, dtype=jnp.int32)
    sel = (lane2[:, None] % (QK_ROPE // 2) == jnp.arange(LANES, dtype=jnp.int32)[None, :] % (QK_ROPE // 2))
    pos_c = jnp.repeat(pos.reshape(t // SUBLANES, SUBLANES), LANES // SUBLANES, axis=1)
    vt_spec = pl.BlockSpec((None, vdim, rows), lambda i: (i // ns, 0, i % ns))
    return pl.pallas_call(
        _mla_in_kernel,
        grid=(t // rows,),
        in_specs=[row(D_MODEL), pl.BlockSpec((rows // SUBLANES, LANES), lambda i: (i, 0))]
        + [_const_spec(s) for s in shapes],
        out_specs=[row(nope), row(rope), row(nope), row(LANES), vt_spec],
        out_shape=[jax.ShapeDtypeStruct((t, nope), BF16), jax.ShapeDtypeStruct((t, rope), BF16),
                   jax.ShapeDtypeStruct((t, nope), BF16), jax.ShapeDtypeStruct((t, LANES), BF16),
                   jax.ShapeDtypeStruct((bsz, vdim, seq), BF16)],
        compiler_params=_params(("parallel",)),
        name="mla_in",
    )(x, pos_c, *ws, rep.astype(BF16), sel.astype(BF16))


def _attn_kernel(qn_ref, qr_ref, kn_ref, kr_ref, vt_ref, o_ref, m_ref, l_ref, acc_ref):
    seq = qn_ref.shape[0]
    blk = min(ATTN_BLOCK, seq)
    p = pl.program_id(1)

    def query_tile(i, carry):
        _attn_query_tile(i, p, blk, qn_ref, qr_ref, kn_ref, kr_ref, vt_ref, o_ref, m_ref, l_ref, acc_ref)
        return carry

    lax.fori_loop(0, seq // blk, query_tile, 0)


def _attn_query_tile(i, p, blk, qn_ref, qr_ref, kn_ref, kr_ref, vt_ref, o_ref, m_ref, l_ref, acc_ref):
    rows = pl.ds(pl.multiple_of(i * blk, blk), blk)
    lane = lax.broadcasted_iota(jnp.int32, (blk, LANES), 1)
    qn = qn_ref[rows, :]
    qr = qr_ref[rows, :]
    zero = jnp.zeros_like(qn)
    rope_lo = (p % 2) * (2 * QK_ROPE)
    qs = []
    for hh in range(2):
        nope_part = jnp.where((lane >= hh * QK_NOPE) & (lane < (hh + 1) * QK_NOPE), qn, zero)
        lo = rope_lo + hh * QK_ROPE
        rope_part = jnp.where((lane >= lo) & (lane < lo + QK_ROPE), qr, zero)
        qs.append(jnp.concatenate([nope_part, rope_part], axis=1))

    m_ref[...] = jnp.full(m_ref.shape, -jnp.inf, F32)
    l_ref[...] = jnp.zeros(l_ref.shape, F32)
    acc_ref[...] = jnp.zeros(acc_ref.shape, F32)

    chains = [(hh, q0) for hh in range(2) for q0 in range(0, blk, ATTN_QSUB)]

    def process(blocks):
        starts = [pl.multiple_of(j * blk, blk) for j, _ in blocks]
        nkeys = lambda masked, q0: q0 + ATTN_QSUB if masked else blk
        def block_scores_of(bi):
            masked, start = blocks[bi][1], starts[bi]
            kfull = jnp.concatenate([kn_ref[pl.ds(start, blk), :], kr_ref[pl.ds(start, blk), :]], axis=1)
            return [_dot_nt(kfull[:nkeys(masked, q0), :], qs[hh][q0:q0 + ATTN_QSUB, :]) for hh, q0 in chains]

        ahead = ATTN_AHEAD
        scores = {bi: block_scores_of(bi) for bi in range(min(ahead, len(blocks)))}
        for bi, ((_, masked), start) in enumerate(zip(blocks, starts)):
            if bi + ahead < len(blocks):
                scores[bi + ahead] = block_scores_of(bi + ahead)
            block_scores = scores.pop(bi)
            vt_blk = vt_ref[:, pl.ds(start, blk)]
            for (hh, q0), s in zip(chains, block_scores):
                qsl = slice(q0, q0 + ATTN_QSUB)
                vt = vt_blk[:, :nkeys(masked, q0)]
                if masked:
                    key = lax.broadcasted_iota(jnp.int32, s.shape, 0)
                    qry = lax.broadcasted_iota(jnp.int32, s.shape, 1) + q0
                    s = jnp.where(key <= qry, s, -jnp.inf)
                m_old = m_ref[hh, :, qsl]
                m_new = jnp.maximum(m_old, jnp.max(s, axis=0, keepdims=True))
                alpha = jnp.exp2(m_old - m_new)
                pr = jnp.exp2(s - m_new)
                l_ref[hh, :, qsl] = alpha * l_ref[hh, :, qsl] + jnp.sum(pr, axis=0, keepdims=True)
                pv = _dot(vt, pr.astype(BF16))
                acc_ref[hh, :, qsl] = (alpha * acc_ref[hh, :, qsl]
                                       + pv[hh * V_DIM:(hh + 1) * V_DIM, :])
                m_ref[hh, :, qsl] = m_new

    grp = ATTN_GROUP
    n_full = i // grp

    def body(jj, carry):
        process([(grp * jj + k, False) for k in range(grp)])
        return carry

    lax.fori_loop(0, n_full, body, 0)

    for rem in range(grp):
        @pl.when(i - grp * n_full == rem)
        def _():
            process([(i - rem + k, False) for k in range(rem)] + [(i, True)])

    ot = jnp.concatenate([acc_ref[0] / l_ref[0], acc_ref[1] / l_ref[1]], axis=0)
    o_ref[rows, :] = ot.T.astype(BF16)


def _attn(qn, qr, kn, kr, vt, bsz, seq):
    t = qn.shape[0]
    blk = min(ATTN_BLOCK, seq)
    npair = MLA_HEADS // 2
    kspec = pl.BlockSpec((seq, LANES), lambda b, p: (b, p))
    qrspec = pl.BlockSpec((seq, LANES), lambda b, p: (b, p // 2))
    krspec = pl.BlockSpec((seq, LANES), lambda b, p: (b, 0))
    vtspec = pl.BlockSpec((None, 2 * V_DIM, seq), lambda b, p: (b, p, 0))
    return pl.pallas_call(
        _attn_kernel,
        grid=(bsz, npair),
        in_specs=[kspec, qrspec, kspec, krspec, vtspec],
        out_specs=kspec,
        out_shape=jax.ShapeDtypeStruct((t, MLA_HEADS * V_DIM), BF16),
        scratch_shapes=[pltpu.VMEM((2, 1, blk), F32), pltpu.VMEM((2, 1, blk), F32),
                        pltpu.VMEM((2, V_DIM, blk), F32)],
        compiler_params=_params(("parallel", "parallel")),
        name="attn",
    )(qn, qr, kn, kr, vt)


def _row(v, width=None):
    v = v.astype(F32).reshape(1, -1)
    if width is not None and v.shape[1] < width:
        v = jnp.pad(v, ((0, 0), (0, width - v.shape[1])))
    return v


def _swap_halves(w):
    k, n = w.shape
    w = w.reshape(k, n // QK_ROPE, 2, QK_ROPE // 2)
    return w[:, :, ::-1, :].reshape(k, n)


def kernel(x, positions, norm_w, ffn_w_gate, ffn_w_up, ffn_w_down, ssm_w_in, ssm_conv_w, ssm_conv_b, ssm_dt_bias, ssm_a_log, ssm_d, ssm_norm_w, ssm_w_out, mla_w_in, mla_q_a_norm, mla_kv_a_norm, mla_w_q_b, mla_w_kv_b, mla_q_norm, mla_k_norm, mla_w_out):
    bsz, seq, _ = x.shape
    t = bsz * seq
    depth = norm_w.shape[0]
    xf = x.reshape(t, D_MODEL)

    def ffn(xin, i, k, attn=None, wo=None):
        return _ffn(xin, _row(norm_w[i, 0 if k == 0 else 2]),
                    ffn_w_gate[i, k].astype(BF16), ffn_w_up[i, k].astype(BF16),
                    ffn_w_down[i, k].astype(BF16), attn, wo)

    for i in range(depth):
        xf = ffn(xf, i, 0)
        j = i // 2
        if i % 2 == 0:
            w_in = ssm_w_in[j]
            wz = w_in[:, :D_INNER].astype(BF16)
            wxbc = w_in[:, D_INNER:D_INNER + CONV_DIM].astype(BF16)
            wdt = jnp.pad(w_in[:, D_INNER + CONV_DIM:], ((0, 0), (0, LANES - SSM_HEADS))).astype(BF16)
            ridx = jnp.arange(CHUNK, dtype=jnp.int32)
            tok = (ridx % SUBLANES) * PERM_ROWS + ridx // SUBLANES
            perm = (tok[:, None] == ridx[None, :]).astype(BF16)
            z, xbc, dt = _ssm_in(xf, bsz, seq, _row(norm_w[i, 1]), perm, wz, wxbc, wdt,
                                 ssm_conv_w[j].astype(F32), _row(ssm_conv_b[j]),
                                 _row(ssm_dt_bias[j], LANES))
            a_row = _row(-jnp.exp(ssm_a_log[j].astype(F32)), LANES)
            d_row = _row(jnp.repeat(ssm_d[j].astype(F32), SSM_HEAD_DIM))
            tril = (tok[None, :] <= tok[:, None]).astype(BF16)
            head_of_col = jnp.arange(D_INNER, dtype=jnp.int32) // SSM_HEAD_DIM
            expand = (jnp.arange(2 * LANES, dtype=jnp.int32)[:, None] % LANES
                      == head_of_col[None, :]).astype(BF16)
            xf = _ssd(xf, bsz, seq, z, xbc, dt, a_row, d_row, _row(ssm_norm_w[j]), tril, perm.T, expand,
                      ssm_w_out[j].astype(BF16))
            xf = ffn(xf, i, 1)
        else:
            w_in = mla_w_in[j]
            wcq = w_in[:, :Q_LORA].astype(BF16)
            wckv = w_in[:, Q_LORA:Q_LORA + KV_LORA].astype(BF16)
            wkr = jnp.tile(w_in[:, Q_LORA + KV_LORA:], (1, LANES // QK_ROPE))
            wq = mla_w_q_b[j].reshape(Q_LORA, MLA_HEADS, QK_DIM)
            wqn = wq[:, :, :QK_NOPE].reshape(Q_LORA, MLA_HEADS * QK_NOPE).astype(BF16)
            wqr = wq[:, :, QK_NOPE:].reshape(Q_LORA, MLA_HEADS * QK_ROPE)
            wkv = mla_w_kv_b[j].reshape(KV_LORA, MLA_HEADS, QK_NOPE + V_DIM)
            wkn = wkv[:, :, :QK_NOPE].reshape(KV_LORA, MLA_HEADS * QK_NOPE).astype(BF16)
            wvt = wkv[:, :, QK_NOPE:].reshape(KV_LORA, MLA_HEADS * V_DIM).T.astype(BF16)
            qg, kg = mla_q_norm[j].astype(F32), mla_k_norm[j].astype(F32)
            tile_n = lambda g: jnp.tile(g, LANES // g.shape[0]).reshape(1, LANES)
            swap = lambda g: jnp.concatenate([g[QK_ROPE // 2:], g[:QK_ROPE // 2]])
            inv_freq = 1.0 / (ROPE_THETA ** (jnp.arange(0, QK_ROPE, 2, dtype=F32) / QK_ROPE))
            freq = jnp.tile(inv_freq, LANES // (QK_ROPE // 2)).reshape(1, LANES)
            sgn = jnp.tile(jnp.concatenate([-jnp.ones(QK_ROPE // 2, F32), jnp.ones(QK_ROPE // 2, F32)]),
                           LANES // QK_ROPE).reshape(1, LANES)
            ws = [_row(norm_w[i, 1]), wcq, wckv, wkr.astype(BF16), _swap_halves(wkr).astype(BF16),
                  _row(mla_q_a_norm[j]), _row(mla_kv_a_norm[j]), wqn, wqr.astype(BF16),
                  _swap_halves(wqr).astype(BF16), wkn, wvt,
                  tile_n(qg[:QK_NOPE]), tile_n(kg[:QK_NOPE]), tile_n(qg[QK_NOPE:]),
                  tile_n(swap(qg[QK_NOPE:])), tile_n(kg[QK_NOPE:]), tile_n(swap(kg[QK_NOPE:])),
                  freq, sgn]
            qn, qr, kn, kr, vt = _mla_in(xf, positions.reshape(t).astype(jnp.int32), bsz, seq, ws)
            o = _attn(qn, qr, kn, kr, vt, bsz, seq)
            xf = ffn(xf, i, 1, attn=o, wo=mla_w_out[j].astype(BF16))
    return xf.reshape(bsz, seq, D_MODEL)
```

```python
import functools
import math

import jax
import jax.numpy as jnp
from jax import lax
from jax.experimental import pallas as pl
from jax.experimental.pallas import tpu as pltpu

F32 = jnp.float32
BF16 = jnp.bfloat16

D_MODEL = 1024
D_FF = 2816
EPS = 1e-6

SSM_HEADS = 32
SSM_HEAD_DIM = 64
SSM_GROUPS = 8
SSM_HPG = SSM_HEADS // SSM_GROUPS
SSM_STATE = 128
D_INNER = SSM_HEADS * SSM_HEAD_DIM
GROUP_COLS = SSM_HPG * SSM_HEAD_DIM
CONV_K = 4
CONV_DIM = D_INNER + 2 * SSM_GROUPS * SSM_STATE
CHUNK = 128

MLA_HEADS = 16
Q_LORA = 384
KV_LORA = 256
QK_NOPE = 64
QK_ROPE = 32
V_DIM = 64
QK_DIM = QK_NOPE + QK_ROPE
ROPE_THETA = 10000.0

LANES = 128
SUBLANES = 8
PERM_ROWS = CHUNK // SUBLANES
VMEM_LIMIT = 56 * 1024 * 1024

FFN_ROWS = 1024
FFN_COLS = 256
SSM_IN_ROWS = 512
SSM_IN_COLS = 256
SSD_ROWS = 4 * CHUNK
MLA_IN_ROWS = 1024
ATTN_BLOCK = 512
ATTN_QSUB = 256
ATTN_AHEAD = 8
ATTN_GROUP = 8


def _dot(a, b):
    return jnp.dot(a, b, preferred_element_type=F32)


def _dot_nt(a, b):
    return lax.dot_general(a, b, (((1,), (1,)), ((), ())), preferred_element_type=F32)


def _dot_tn(a, b):
    return lax.dot_general(a, b, (((0,), (0,)), ((), ())), preferred_element_type=F32)


def _rms(x, w):
    return x * lax.rsqrt(jnp.mean(x * x, axis=-1, keepdims=True) + EPS) * w


def _silu(x):
    hx = 0.5 * x
    return hx * jnp.tanh(hx) + hx


def _split2(x):
    hi = x.astype(BF16)
    lo = (x - hi.astype(F32)).astype(BF16)
    return hi, lo


def _split3(x):
    hi = x.astype(BF16)
    r = x - hi.astype(F32)
    mid = r.astype(BF16)
    lo = (r - mid.astype(F32)).astype(BF16)
    return hi, mid, lo


def _const_spec(shape):
    zeros = (0,) * len(shape)
    return pl.BlockSpec(shape, lambda *_: zeros, pipeline_mode=pl.Buffered(1))


def _params(sem):
    return pltpu.CompilerParams(dimension_semantics=sem, vmem_limit_bytes=VMEM_LIMIT)


def _ffn_body(x, nw_ref, wg_ref, wu_ref, wd_ref, o_ref, acc_ref):
    h = _rms(x, nw_ref[...]).astype(BF16)
    for c in range(D_FF // FFN_COLS):
        sl = slice(c * FFN_COLS, (c + 1) * FFN_COLS)
        g = _dot(h, wg_ref[:, sl])
        u = _dot(h, wu_ref[:, sl])
        a = (_silu(g) * u).astype(BF16)
        contrib = _dot(a, wd_ref[sl, :])
        if c == 0:
            acc_ref[...] = contrib
        else:
            acc_ref[...] += contrib
    o_ref[...] = x + 0.5 * acc_ref[...]


def _ffn_kernel(x_ref, nw_ref, wg_ref, wu_ref, wd_ref, o_ref, acc_ref):
    _ffn_body(x_ref[...], nw_ref, wg_ref, wu_ref, wd_ref, o_ref, acc_ref)


def _ffn_proj_kernel(x_ref, a_ref, wo_ref, nw_ref, wg_ref, wu_ref, wd_ref, o_ref, acc_ref):
    x = x_ref[...] + _dot(a_ref[...], wo_ref[...])
    _ffn_body(x, nw_ref, wg_ref, wu_ref, wd_ref, o_ref, acc_ref)


def _ffn(x, nw, wg, wu, wd, attn=None, wo=None):
    t = x.shape[0]
    rows = min(FFN_ROWS, t)
    row_spec = pl.BlockSpec((rows, D_MODEL), lambda i: (i, 0))
    w_specs = [_const_spec((1, D_MODEL)), _const_spec((D_MODEL, D_FF)),
               _const_spec((D_MODEL, D_FF)), _const_spec((D_FF, D_MODEL))]
    if attn is None:
        kern, ins, specs = _ffn_kernel, (x,), [row_spec]
    else:
        kern, ins = _ffn_proj_kernel, (x, attn, wo)
        specs = [row_spec, row_spec, _const_spec((D_MODEL, D_MODEL))]
    return pl.pallas_call(
        kern,
        grid=(t // rows,),
        in_specs=specs + w_specs,
        out_specs=row_spec,
        out_shape=jax.ShapeDtypeStruct((t, D_MODEL), F32),
        scratch_shapes=[pltpu.VMEM((rows, D_MODEL), F32)],
        compiler_params=_params(("parallel",)),
        name="ffn" if attn is None else "ffn_proj",
    )(*ins, nw, wg, wu, wd)


def _ssm_in_kernel(x_ref, nw_ref, perm_ref, wz_ref, wxbc_ref, wdt_ref, cw_ref, cb_ref, dtb_ref,
                   z_ref, xbc_ref, dt_ref, carry):
    rows = x_ref.shape[0]
    nchunk = rows // CHUNK
    ntail = CONV_K - 1
    h_tok = _rms(x_ref[...], nw_ref[...]).astype(BF16)
    h = jnp.concatenate([_dot(perm_ref[...], h_tok[ch * CHUNK:(ch + 1) * CHUNK, :]).astype(BF16)
                         for ch in range(nchunk)], axis=0)

    @pl.when(pl.program_id(1) == 0)
    def _():
        carry[...] = jnp.zeros(carry.shape, F32)

    sub = lax.broadcasted_iota(jnp.int32, (SUBLANES, SSM_IN_COLS), 0)
    n_conv = CONV_DIM // SSM_IN_COLS
    n_z = D_INNER // SSM_IN_COLS
    for c in range(n_conv):
        sl = slice(c * SSM_IN_COLS, (c + 1) * SSM_IN_COLS)
        u = _dot(h, wxbc_ref[:, sl])
        if c % (n_conv // n_z) == 0:
            zsl = slice(c // (n_conv // n_z) * SSM_IN_COLS, (c // (n_conv // n_z) + 1) * SSM_IN_COLS)
            z_ref[:, zsl] = _dot(h, wz_ref[:, zsl]).astype(BF16)
        vrow = [[u[(ch * PERM_ROWS + i) * SUBLANES:(ch * PERM_ROWS + i + 1) * SUBLANES, :]
                 for i in range(PERM_ROWS)] for ch in range(nchunk)]
        wrapped = []
        for ch in range(nchunk):
            wr = {}
            for k in range(PERM_ROWS - ntail, PERM_ROWS):
                kk = k - (PERM_ROWS - ntail)
                prev = carry[kk * SUBLANES:(kk + 1) * SUBLANES, sl] if ch == 0 else vrow[ch - 1][k]
                wr[k] = jnp.where(sub == 0, pltpu.roll(prev, 1, axis=0), pltpu.roll(vrow[ch][k], 1, axis=0))
            wrapped.append(wr)
        acc = cb_ref[:, sl] + cw_ref[CONV_K - 1:CONV_K, sl] * u
        for back in range(1, CONV_K):
            shifted = jnp.concatenate(
                [vrow[ch][i - back] if i >= back else wrapped[ch][PERM_ROWS + i - back]
                 for ch in range(nchunk) for i in range(PERM_ROWS)], axis=0)
            acc = acc + cw_ref[CONV_K - 1 - back:CONV_K - back, sl] * shifted
        xbc_ref[:, sl] = _silu(acc).astype(BF16)
        for kk in range(ntail):
            carry[kk * SUBLANES:(kk + 1) * SUBLANES, sl] = vrow[nchunk - 1][PERM_ROWS - ntail + kk]
    dtv = _dot(h, wdt_ref[...]) + dtb_ref[...]
    dt_ref[...] = jnp.maximum(dtv, 0.0) + jnp.log1p(jnp.exp(-jnp.abs(dtv)))


def _ssm_in(x, bsz, seq, nw, perm, wz, wxbc, wdt, cw, cb, dtb):
    t = x.shape[0]
    rows = min(SSM_IN_ROWS, seq)
    ns = seq // rows
    row = lambda w: pl.BlockSpec((rows, w), lambda b, s: (b * ns + s, 0))
    return pl.pallas_call(
        _ssm_in_kernel,
        grid=(bsz, ns),
        in_specs=[row(D_MODEL), _const_spec((1, D_MODEL)), _const_spec((CHUNK, CHUNK)),
                  _const_spec((D_MODEL, D_INNER)),
                  _const_spec((D_MODEL, CONV_DIM)), _const_spec((D_MODEL, LANES)),
                  _const_spec((CONV_K, CONV_DIM)), _const_spec((1, CONV_DIM)),
                  _const_spec((1, LANES))],
        out_specs=[row(D_INNER), row(CONV_DIM), row(LANES)],
        out_shape=[jax.ShapeDtypeStruct((t, D_INNER), BF16),
                   jax.ShapeDtypeStruct((t, CONV_DIM), BF16),
                   jax.ShapeDtypeStruct((t, LANES), F32)],
        scratch_shapes=[pltpu.VMEM(((CONV_K - 1) * SUBLANES, CONV_DIM), F32)],
        compiler_params=_params(("parallel", "arbitrary")),
        name="ssm_in",
    )(x, nw, perm, wz, wxbc, wdt, cw, cb, dtb)


def _ssd_kernel(x_ref, z_ref, xbc_ref, dt_ref, a_ref, d_ref, nw_ref, tril_ref, unperm_ref, exp_ref,
                wo_ref, o_ref, state, ybuf):
    L = CHUNK

    @pl.when(pl.program_id(1) == 0)
    def _():
        state[...] = jnp.zeros(state.shape, F32)

    row = lax.broadcasted_iota(jnp.int32, (L, L), 0)
    col = lax.broadcasted_iota(jnp.int32, (L, L), 1)
    tok = lambda r: (r % SUBLANES) * PERM_ROWS + r // SUBLANES
    causal = tok(col) <= tok(row)
    head_of_lane = lax.broadcasted_iota(jnp.int32, (L, GROUP_COLS), 1) // SSM_HEAD_DIM

    for ck in range(x_ref.shape[0] // L):
        rsl = slice(ck * L, (ck + 1) * L)
        dt = dt_ref[rsl, :]
        dta = dt * a_ref[...]
        hi, mid, lo = _split3(dta)
        cs = _dot(tril_ref[...], jnp.concatenate([hi, mid, lo], axis=1))
        acs = cs[:, :LANES] + cs[:, LANES:2 * LANES] + cs[:, 2 * LANES:]
        acs_last = acs[L - 1:L, :]
        eacs = jnp.exp(acs)
        wgt = dt * jnp.exp(acs_last - acs)
        acs_src = (acs - jnp.log(dt)).T

        wgt_s = jnp.concatenate(_split2(wgt), axis=1)
        eacs_s = jnp.concatenate(_split2(eacs), axis=1)

        def group_front(g):
            gsl = slice(g * GROUP_COLS, (g + 1) * GROUP_COLS)
            wgt_x = _dot(wgt_s, exp_ref[:, gsl])
            eacs_x = _dot(eacs_s, exp_ref[:, gsl])
            xs_b = xbc_ref[rsl, gsl]
            xs = xs_b.astype(F32)
            bm = xbc_ref[rsl, D_INNER + g * SSM_STATE:D_INNER + (g + 1) * SSM_STATE]
            cm = xbc_ref[rsl, D_INNER + SSM_GROUPS * SSM_STATE + g * SSM_STATE:
                         D_INNER + SSM_GROUPS * SSM_STATE + (g + 1) * SSM_STATE]
            cb = _dot_nt(cm, bm)
            st = state[g]
            y_off = _dot(cm, st.astype(BF16)) * eacs_x
            state[g] = st * eacs_x[L - 1:L, :] + _dot_tn(bm, (xs * wgt_x).astype(BF16))
            return cb, y_off, xs_b, xs

        def group_back(g, cb, y_off, xs_b, xs):
            gsl = slice(g * GROUP_COLS, (g + 1) * GROUP_COLS)
            ms = []
            for r in range(SSM_HPG):
                hd = g * SSM_HPG + r
                seg = acs[:, hd:hd + 1] - acs_src[hd:hd + 1, :]
                decay_dt = jnp.exp(jnp.where(causal, seg, -jnp.inf))
                ms.append((cb * decay_dt).astype(BF16))
            x_diag = jnp.concatenate(
                [jnp.where(head_of_lane == r, xs_b, jnp.zeros_like(xs_b)) for r in range(SSM_HPG)], axis=0)
            y = _dot(jnp.concatenate(ms, axis=1), x_diag) + y_off + d_ref[:, gsl] * xs
            gt = y * _silu(z_ref[rsl, gsl].astype(F32))
            gt = gt * lax.rsqrt(jnp.mean(gt * gt, axis=-1, keepdims=True) + EPS)
            ybuf[rsl, gsl] = (gt * nw_ref[:, gsl]).astype(BF16)

        ahead = 1
        fronts = {g: group_front(g) for g in range(min(ahead, SSM_GROUPS))}
        for g in range(SSM_GROUPS):
            if g + ahead < SSM_GROUPS:
                fronts[g + ahead] = group_front(g + ahead)
            group_back(g, *fronts.pop(g))
        ybuf[rsl, :] = _dot(unperm_ref[...], ybuf[rsl, :]).astype(BF16)
    o_ref[...] = x_ref[...] + _dot(ybuf[...], wo_ref[...])


def _ssd(x, bsz, seq, z, xbc, dt, a_row, d_row, nw, tril, unperm, expand, wo):
    t = x.shape[0]
    rows = min(SSD_ROWS, seq)
    nc = seq // rows
    row = lambda w: pl.BlockSpec((rows, w), lambda b, c: (b * nc + c, 0))
    return pl.pallas_call(
        _ssd_kernel,
        grid=(bsz, nc),
        in_specs=[row(D_MODEL), row(D_INNER), row(CONV_DIM), row(LANES),
                  _const_spec((1, LANES)), _const_spec((1, D_INNER)), _const_spec((1, D_INNER)),
                  _const_spec((CHUNK, CHUNK)), _const_spec((CHUNK, CHUNK)), _const_spec((2 * LANES, D_INNER)),
                  _const_spec((D_INNER, D_MODEL))],
        out_specs=row(D_MODEL),
        out_shape=jax.ShapeDtypeStruct((t, D_MODEL), F32),
        scratch_shapes=[pltpu.VMEM((SSM_GROUPS, SSM_STATE, GROUP_COLS), F32),
                        pltpu.VMEM((rows, D_INNER), BF16)],
        compiler_params=_params(("parallel", "arbitrary")),
        name="ssd",
    )(x, z, xbc, dt, a_row, d_row, nw, tril, unperm, expand, wo)


def _group_rstd(sq, group):
    lane = lax.broadcasted_iota(jnp.int32, sq.shape, 1)
    res = jnp.zeros_like(sq)
    for gi in range(LANES // group):
        msk = (lane >= gi * group) & (lane < (gi + 1) * group)
        s = jnp.sum(jnp.where(msk, sq, 0.0), axis=-1, keepdims=True)
        res = jnp.where(msk, lax.rsqrt(s * (1.0 / group) + EPS), res)
    return res


def _mla_in_kernel(x_ref, pos_ref, nw_ref, wcq_ref, wckv_ref, wkr_ref, wkrs_ref,
                   qan_ref, kvan_ref, wqn_ref, wqr_ref, wqrs_ref, wkn_ref, wvt_ref,
                   gqn_ref, gkn_ref, gqr_ref, gqrs_ref, gkr_ref, gkrs_ref, freq_ref, sgn_ref,
                   rep_ref, sel_ref, qn_ref, qr_ref, kn_ref, kr_ref, vt_ref):
    h = _rms(x_ref[...], nw_ref[...]).astype(BF16)
    cq =_rms(_dot(h, wcq_ref[...]), qan_ref[...]).astype(BF16)
    ckv = _rms(_dot(h, wckv_ref[...]), kvan_ref[...]).astype(BF16)

    rows = x_ref.shape[0]
    ang = pos_ref[...].astype(F32) * freq_ref[...]
    tab = jnp.concatenate([jnp.cos(ang), jnp.sin(ang)], axis=1)
    rep = _dot(rep_ref[...], jnp.concatenate(_split2(tab), axis=0))
    tok8 = lax.broadcasted_iota(jnp.int32, (rows, 2 * LANES), 0) % SUBLANES
    grp = (lax.broadcasted_iota(jnp.int32, (rows, 2 * LANES), 1) % LANES) // (QK_ROPE // 2)
    own = jnp.where(tok8 == grp, rep, 0.0)
    cos = _dot(jnp.concatenate(_split2(own[:, :LANES]), axis=1), sel_ref[...])
    sin = _dot(jnp.concatenate(_split2(own[:, LANES:]), axis=1), sel_ref[...]) * sgn_ref[...]

    qscale = QK_DIM ** -0.5 * math.log2(math.e)

    vt_ref[...] = _dot_nt(wvt_ref[...], ckv).astype(BF16)
    for c in range(MLA_HEADS * QK_NOPE // LANES):
        sl = slice(c * LANES, (c + 1) * LANES)
        qn = _dot(cq, wqn_ref[:, sl])
        qn_ref[:, sl] = (qn * _group_rstd(qn * qn, QK_NOPE) * (gqn_ref[...] * qscale)).astype(BF16)
        kn = _dot(ckv, wkn_ref[:, sl])
        kn_ref[:, sl] = (kn * _group_rstd(kn * kn, QK_NOPE) * gkn_ref[...]).astype(BF16)
    for c in range(MLA_HEADS * QK_ROPE // LANES):
        sl = slice(c * LANES, (c + 1) * LANES)
        qr = _dot(cq, wqr_ref[:, sl])
        qrs = _dot(cq, wqrs_ref[:, sl])
        rot = qr * gqr_ref[...] * cos + qrs * gqrs_ref[...] * sin
        qr_ref[:, sl] = (rot * _group_rstd(qr * qr, QK_ROPE) * qscale).astype(BF16)
    kr = _dot(h, wkr_ref[...])
    krs = _dot(h, wkrs_ref[...])
    rot = kr * gkr_ref[...] * cos + krs * gkrs_ref[...] * sin
    kr_ref[...] = (rot * _group_rstd(kr * kr, QK_ROPE)).astype(BF16)


def _mla_in(x, pos, bsz, seq, ws):
    t = x.shape[0]
    rows = min(MLA_IN_ROWS, seq)
    ns = seq // rows
    row = lambda w: pl.BlockSpec((rows, w), lambda i: (i, 0))
    nope = MLA_HEADS * QK_NOPE
    rope = MLA_HEADS * QK_ROPE
    vdim = MLA_HEADS * V_DIM
    shapes = [(1, D_MODEL), (D_MODEL, Q_LORA), (D_MODEL, KV_LORA), (D_MODEL, LANES), (D_MODEL, LANES),
              (1, Q_LORA), (1, KV_LORA), (Q_LORA, nope), (Q_LORA, rope), (Q_LORA, rope),
              (KV_LORA, nope), (vdim, KV_LORA)] + [(1, LANES)] * 8 + [(rows, rows // 4), (2 * LANES, LANES)]
    tok = jnp.arange(rows, dtype=jnp.int32)
    rep = (tok[:, None] // SUBLANES == jnp.arange(rows // 4, dtype=jnp.int32)[None, :] % (rows // SUBLANES))
    lane2 = jnp.arange(2 * LANES, dtype=jnp.int32)
    sel = (lane2[:, None] % (QK_ROPE // 2) == jnp.arange(LANES, dtype=jnp.int32)[None, :] % (QK_ROPE // 2))
    pos_c = jnp.repeat(pos.reshape(t // SUBLANES, SUBLANES), LANES // SUBLANES, axis=1)
    vt_spec = pl.BlockSpec((None, vdim, rows), lambda i: (i // ns, 0, i % ns))
    return pl.pallas_call(
        _mla_in_kernel,
        grid=(t // rows,),
        in_specs=[row(D_MODEL), pl.BlockSpec((rows // SUBLANES, LANES), lambda i: (i, 0))]
        + [_const_spec(s) for s in shapes],
        out_specs=[row(nope), row(rope), row(nope), row(LANES), vt_spec],
        out_shape=[jax.ShapeDtypeStruct((t, nope), BF16), jax.ShapeDtypeStruct((t, rope), BF16),
                   jax.ShapeDtypeStruct((t, nope), BF16), jax.ShapeDtypeStruct((t, LANES), BF16),
                   jax.ShapeDtypeStruct((bsz, vdim, seq), BF16)],
        compiler_params=_params(("parallel",)),
        name="mla_in",
    )(x, pos_c, *ws, rep.astype(BF16), sel.astype(BF16))


def _attn_kernel(qn_ref, qr_ref, kn_ref, kr_ref, vt_ref, o_ref, m_ref, l_ref, acc_ref):
    seq = qn_ref.shape[0]
    blk = min(ATTN_BLOCK, seq)
    p = pl.program_id(1)

    def query_tile(i, carry):
        _attn_query_tile(i, p, blk, qn_ref, qr_ref, kn_ref, kr_ref, vt_ref, o_ref, m_ref, l_ref, acc_ref)
        return carry

    lax.fori_loop(0, seq // blk, query_tile, 0)


def _attn_query_tile(i, p, blk, qn_ref, qr_ref, kn_ref, kr_ref, vt_ref, o_ref, m_ref, l_ref, acc_ref):
    rows = pl.ds(pl.multiple_of(i * blk, blk), blk)
    lane = lax.broadcasted_iota(jnp.int32, (blk, LANES), 1)
    qn = qn_ref[rows, :]
    qr = qr_ref[rows, :]
    zero = jnp.zeros_like(qn)
    rope_lo = (p % 2) * (2 * QK_ROPE)
    qs = []
    for hh in range(2):
        nope_part = jnp.where((lane >= hh * QK_NOPE) & (lane < (hh + 1) * QK_NOPE), qn, zero)
        lo = rope_lo + hh * QK_ROPE
        rope_part = jnp.where((lane >= lo) & (lane < lo + QK_ROPE), qr, zero)
        qs.append(jnp.concatenate([nope_part, rope_part], axis=1))

    m_ref[...] = jnp.full(m_ref.shape, -jnp.inf, F32)
    l_ref[...] = jnp.zeros(l_ref.shape, F32)
    acc_ref[...] = jnp.zeros(acc_ref.shape, F32)

    chains = [(hh, q0) for hh in range(2) for q0 in range(0, blk, ATTN_QSUB)]

    def process(blocks):
        starts = [pl.multiple_of(j * blk, blk) for j, _ in blocks]
        nkeys = lambda masked, q0: q0 + ATTN_QSUB if masked else blk
        def block_scores_of(bi):
            masked, start = blocks[bi][1], starts[bi]
            kfull = jnp.concatenate([kn_ref[pl.ds(start, blk), :], kr_ref[pl.ds(start, blk), :]], axis=1)
            return [_dot_nt(kfull[:nkeys(masked, q0), :], qs[hh][q0:q0 + ATTN_QSUB, :]) for hh, q0 in chains]

        ahead = ATTN_AHEAD
        scores = {bi: block_scores_of(bi) for bi in range(min(ahead, len(blocks)))}
        for bi, ((_, masked), start) in enumerate(zip(blocks, starts)):
            if bi + ahead < len(blocks):
                scores[bi + ahead] = block_scores_of(bi + ahead)
            block_scores = scores.pop(bi)
            vt_blk = vt_ref[:, pl.ds(start, blk)]
            for (hh, q0), s in zip(chains, block_scores):
                qsl = slice(q0, q0 + ATTN_QSUB)
                vt = vt_blk[:, :nkeys(masked, q0)]
                if masked:
                    key = lax.broadcasted_iota(jnp.int32, s.shape, 0)
                    qry = lax.broadcasted_iota(jnp.int32, s.shape, 1) + q0
                    s = jnp.where(key <= qry, s, -jnp.inf)
                m_old = m_ref[hh, :, qsl]
                m_new = jnp.maximum(m_old, jnp.max(s, axis=0, keepdims=True))
                alpha = jnp.exp2(m_old - m_new)
                pr = jnp.exp2(s - m_new)
                l_ref[hh, :, qsl] = alpha * l_ref[hh, :, qsl] + jnp.sum(pr, axis=0, keepdims=True)
                pv = _dot(vt, pr.astype(BF16))
                acc_ref[hh, :, qsl] = (alpha * acc_ref[hh, :, qsl]
                                       + pv[hh * V_DIM:(hh + 1) * V_DIM, :])
                m_ref[hh, :, qsl] = m_new

    grp = ATTN_GROUP
    n_full = i // grp

    def body(jj, carry):
        process([(grp * jj + k, False) for k in range(grp)])
        return carry

    lax.fori_loop(0, n_full, body, 0)

    for rem in range(grp):
        @pl.when(i - grp * n_full == rem)
        def _():
            process([(i - rem + k, False) for k in range(rem)] + [(i, True)])

    ot = jnp.concatenate([acc_ref[0] / l_ref[0], acc_ref[1] / l_ref[1]], axis=0)
    o_ref[rows, :] = ot.T.astype(BF16)


def _attn(qn, qr, kn, kr, vt, bsz, seq):
    t = qn.shape[0]
    blk = min(ATTN_BLOCK, seq)
    npair = MLA_HEADS // 2
    kspec = pl.BlockSpec((seq, LANES), lambda b, p: (b, p))
    qrspec = pl.BlockSpec((seq, LANES), lambda b, p: (b, p // 2))
    krspec = pl.BlockSpec((seq, LANES), lambda b, p: (b, 0))
    vtspec = pl.BlockSpec((None, 2 * V_DIM, seq), lambda b, p: (b, p, 0))
    return pl.pallas_call(
        _attn_kernel,
        grid=(bsz, npair),
        in_specs=[kspec, qrspec, kspec, krspec, vtspec],
        out_specs=kspec,
        out_shape=jax.ShapeDtypeStruct((t, MLA_HEADS * V_DIM), BF16),
        scratch_shapes=[pltpu.VMEM((2, 1, blk), F32), pltpu.VMEM((2, 1, blk), F32),
                        pltpu.VMEM((2, V_DIM, blk), F32)],
        compiler_params=_params(("parallel", "parallel")),
        name="attn",
    )(qn, qr, kn, kr, vt)


def _row(v, width=None):
    v = v.astype(F32).reshape(1, -1)
    if width is not None and v.shape[1] < width:
        v = jnp.pad(v, ((0, 0), (0, width - v.shape[1])))
    return v


def _swap_halves(w):
    k, n = w.shape
    w = w.reshape(k, n // QK_ROPE, 2, QK_ROPE // 2)
    return w[:, :, ::-1, :].reshape(k, n)


def kernel(x, positions, norm_w, ffn_w_gate, ffn_w_up, ffn_w_down, ssm_w_in, ssm_conv_w, ssm_conv_b, ssm_dt_bias, ssm_a_log, ssm_d, ssm_norm_w, ssm_w_out, mla_w_in, mla_q_a_norm, mla_kv_a_norm, mla_w_q_b, mla_w_kv_b, mla_q_norm, mla_k_norm, mla_w_out):
    bsz, seq, _ = x.shape
    t = bsz * seq
    depth = norm_w.shape[0]
    xf = x.reshape(t, D_MODEL)

    def ffn(xin, i, k, attn=None, wo=None):
        return _ffn(xin, _row(norm_w[i, 0 if k == 0 else 2]),
                    ffn_w_gate[i, k].astype(BF16), ffn_w_up[i, k].astype(BF16),
                    ffn_w_down[i, k].astype(BF16), attn, wo)

    for i in range(depth):
        xf = ffn(xf, i, 0)
        j = i // 2
        if i % 2 == 0:
            w_in = ssm_w_in[j]
            wz = w_in[:, :D_INNER].astype(BF16)
            wxbc = w_in[:, D_INNER:D_INNER + CONV_DIM].astype(BF16)
            wdt = jnp.pad(w_in[:, D_INNER + CONV_DIM:], ((0, 0), (0, LANES - SSM_HEADS))).astype(BF16)
            ridx = jnp.arange(CHUNK, dtype=jnp.int32)
            tok = (ridx % SUBLANES) * PERM_ROWS + ridx // SUBLANES
            perm = (tok[:, None] == ridx[None, :]).astype(BF16)
            z, xbc, dt = _ssm_in(xf, bsz, seq, _row(norm_w[i, 1]), perm, wz, wxbc, wdt,
                                 ssm_conv_w[j].astype(F32), _row(ssm_conv_b[j]),
                                 _row(ssm_dt_bias[j], LANES))
            a_row = _row(-jnp.exp(ssm_a_log[j].astype(F32)), LANES)
            d_row = _row(jnp.repeat(ssm_d[j].astype(F32), SSM_HEAD_DIM))
            tril = (tok[None, :] <= tok[:, None]).astype(BF16)
            head_of_col = jnp.arange(D_INNER, dtype=jnp.int32) // SSM_HEAD_DIM
            expand = (jnp.arange(2 * LANES, dtype=jnp.int32)[:, None] % LANES
                      == head_of_col[None, :]).astype(BF16)
            xf = _ssd(xf, bsz, seq, z, xbc, dt, a_row, d_row, _row(ssm_norm_w[j]), tril, perm.T, expand,
                      ssm_w_out[j].astype(BF16))
            xf = ffn(xf, i, 1)
        else:
            w_in = mla_w_in[j]
            wcq = w_in[:, :Q_LORA].astype(BF16)
            wckv = w_in[:, Q_LORA:Q_LORA + KV_LORA].astype(BF16)
            wkr = jnp.tile(w_in[:, Q_LORA + KV_LORA:], (1, LANES // QK_ROPE))
            wq = mla_w_q_b[j].reshape(Q_LORA, MLA_HEADS, QK_DIM)
            wqn = wq[:, :, :QK_NOPE].reshape(Q_LORA, MLA_HEADS * QK_NOPE).astype(BF16)
            wqr = wq[:, :, QK_NOPE:].reshape(Q_LORA, MLA_HEADS * QK_ROPE)
            wkv = mla_w_kv_b[j].reshape(KV_LORA, MLA_HEADS, QK_NOPE + V_DIM)
            wkn = wkv[:, :, :QK_NOPE].reshape(KV_LORA, MLA_HEADS * QK_NOPE).astype(BF16)
            wvt = wkv[:, :, QK_NOPE:].reshape(KV_LORA, MLA_HEADS * V_DIM).T.astype(BF16)
            qg, kg = mla_q_norm[j].astype(F32), mla_k_norm[j].astype(F32)
            tile_n = lambda g: jnp.tile(g, LANES // g.shape[0]).reshape(1, LANES)
            swap = lambda g: jnp.concatenate([g[QK_ROPE // 2:], g[:QK_ROPE // 2]])
            inv_freq = 1.0 / (ROPE_THETA ** (jnp.arange(0, QK_ROPE, 2, dtype=F32) / QK_ROPE))
            freq = jnp.tile(inv_freq, LANES // (QK_ROPE // 2)).reshape(1, LANES)
            sgn = jnp.tile(jnp.concatenate([-jnp.ones(QK_ROPE // 2, F32), jnp.ones(QK_ROPE // 2, F32)]),
                           LANES // QK_ROPE).reshape(1, LANES)
            ws = [_row(norm_w[i, 1]), wcq, wckv, wkr.astype(BF16), _swap_halves(wkr).astype(BF16),
                  _row(mla_q_a_norm[j]), _row(mla_kv_a_norm[j]), wqn, wqr.astype(BF16),
                  _swap_halves(wqr).astype(BF16), wkn, wvt,
                  tile_n(qg[:QK_NOPE]), tile_n(kg[:QK_NOPE]), tile_n(qg[QK_NOPE:]),
                  tile_n(swap(qg[QK_NOPE:])), tile_n(kg[QK_NOPE:]), tile_n(swap(kg[QK_NOPE:])),
                  freq, sgn]
            qn, qr, kn, kr, vt = _mla_in(xf, positions.reshape(t).astype(jnp.int32), bsz, seq, ws)
            o = _attn(qn, qr, kn, kr, vt, bsz, seq)
            xf = ffn(xf, i, 1, attn=o, wo=mla_w_out[j].astype(BF16))
    return xf.reshape(bsz, seq, D_MODEL)
```
